```python
import math
import jax, jax.numpy as jnp
from jax import lax
import numpy as np

D_MODEL = 2048
BATCH = 4
SEQ = 2048
DEPTH = 2
DEC_BATCH = 128
DEC_SEQ = 8
PAST_LEN = 16384
PAGE_SIZE = 128

Q_BLOCK = 128
NSA_DH = 64
NSA_HEADS = D_MODEL // 4 // NSA_DH
NSA_BLOCK = 64
NSA_TOPK = 16
NSA_WINDOW = 512
NSA_CMP_HID = 2 * NSA_DH
NSA_FORCED = 1e4
NSA_SCALE = NSA_DH ** -0.5
DIFF_DH = 64
DIFF_DV = 2 * DIFF_DH
DIFF_HEADS = D_MODEL // 4 // DIFF_DV
DIFF_KV = 2 * DIFF_DH + DIFF_DV
DIFF_SCALE = DIFF_DH ** -0.5
MLA_DV = 128
MLA_HEADS = D_MODEL // 2 // MLA_DV
MLA_Q_LORA = D_MODEL * 3 // 16
MLA_KV_LORA = D_MODEL // 16
MLA_NOPE = 64
MLA_ROPE = 32
MLA_KV = MLA_KV_LORA + MLA_ROPE
MLA_SCALE = (MLA_NOPE + MLA_ROPE) ** -0.5
ROPE_BASE = 10000.0
MIX_WIDTH = NSA_HEADS * NSA_DH + DIFF_HEADS * DIFF_DV + MLA_HEADS * MLA_DV
N_GROUPS = 4
EXP_PER_GROUP = 8
N_EXPERTS = N_GROUPS * EXP_PER_GROUP
EXP_HID = D_MODEL // 8
TOPK_IN_GROUP = 2
LN_EPS = 1e-5
RMS_EPS = 1e-6
DEEPNORM_ALPHA = (2 * DEPTH) ** 0.25
DEEPNORM_BETA = (8 * DEPTH) ** -0.25
NEG_INF = -1e30

IN_SPLITS = (
    ('nsa_q', NSA_HEADS * NSA_DH),
    ('nsa_cmp_k', NSA_DH), ('nsa_cmp_v', NSA_DH),
    ('nsa_sel_k', NSA_DH), ('nsa_sel_v', NSA_DH),
    ('nsa_win_k', NSA_DH), ('nsa_win_v', NSA_DH),
    ('nsa_gate', NSA_HEADS * 3),
    ('diff_q', DIFF_HEADS * 2 * DIFF_DH), ('diff_k', 2 * DIFF_DH), ('diff_v', DIFF_DV),
    ('mla_cq', MLA_Q_LORA), ('mla_ckv', MLA_KV_LORA), ('mla_kr', MLA_ROPE),
)
IN_COLS = sum(n for _, n in IN_SPLITS)

kernel_name = 'hymba_nsa_diff_mla_hmoe_step'


def split_in(h):
    bounds = np.cumsum([n for _, n in IN_SPLITS])[:-1].tolist()
    return dict(zip([name for name, _ in IN_SPLITS], jnp.split(h, bounds, axis=-1)))


def layer_norm(x, g, b):
    xf = x.astype(jnp.float32)
    mu = jnp.mean(xf, axis=-1, keepdims=True)
    var = jnp.mean(jnp.square(xf - mu), axis=-1, keepdims=True)
    return ((xf - mu) * lax.rsqrt(var + LN_EPS)).astype(x.dtype) * g + b


def rms_norm(x, g):
    xf = x.astype(jnp.float32)
    return (xf * lax.rsqrt(jnp.mean(xf * xf, axis=-1, keepdims=True) + RMS_EPS)).astype(x.dtype) * g


def masked_softmax(s, mask):
    s = jnp.where(mask, s, NEG_INF)
    e = jnp.exp(s - jnp.max(s, axis=-1, keepdims=True)) * mask
    return e / jnp.maximum(jnp.sum(e, axis=-1, keepdims=True), 1e-30)


def alibi_slopes(n):
    return jnp.asarray([2.0 ** (-8.0 * (i + 1) / n) for i in range(n)], jnp.float32)


def apply_rope(x, pos):
    half = x.shape[-1] // 2
    inv = ROPE_BASE ** (-jnp.arange(half, dtype=jnp.float32) / half)
    ang = pos.astype(jnp.float32)[:, None] * inv[None, :]
    shape = (1, pos.shape[0]) + (1,) * (x.ndim - 3) + (half,)
    cos, sin = jnp.cos(ang).reshape(shape), jnp.sin(ang).reshape(shape)
    x1, x2 = x[..., :half].astype(jnp.float32), x[..., half:].astype(jnp.float32)
    return jnp.concatenate([x1 * cos - x2 * sin, x2 * cos + x1 * sin], axis=-1).astype(x.dtype)


def map_query_blocks(fn, q_tensors, q_pos):
    T = q_pos.shape[0]
    blk = math.gcd(T, Q_BLOCK)
    nb = T // blk
    qs = tuple(t.reshape((t.shape[0], nb, blk) + t.shape[2:]).swapaxes(0, 1) for t in q_tensors)
    out = lax.map(lambda a: fn(*a[0], a[1]), (qs, q_pos.reshape(nb, blk)))
    return out.swapaxes(0, 1).reshape((out.shape[1], T) + out.shape[3:])


def paged_rows(cache, layer, page_table):
    g = cache[layer, page_table]
    return g.reshape((g.shape[0], g.shape[1] * g.shape[2]) + g.shape[3:])


def compress(rows, pe, w1, w2):
    B, S, dh = rows.shape
    nb = S // NSA_BLOCK
    blk = rows[:, :nb * NSA_BLOCK].reshape(B, nb, NSA_BLOCK, dh) + pe
    h = jax.nn.gelu(blk.reshape(B, nb, NSA_BLOCK * dh) @ w1)
    return h @ w2


def nsa_mixer(parts, pos0, pos, lw, nsa_past, win_past):
    B, T = parts['nsa_q'].shape[:2]
    q = parts['nsa_q'].reshape(B, T, NSA_HEADS, NSA_DH)
    gates = jax.nn.sigmoid(parts['nsa_gate'].astype(jnp.float32)).astype(q.dtype).reshape(B, T, NSA_HEADS, 3)
    rows = jnp.stack([parts['nsa_cmp_k'], parts['nsa_cmp_v'], parts['nsa_sel_k'], parts['nsa_sel_v']], axis=2)
    win_rows = jnp.stack([parts['nsa_win_k'], parts['nsa_win_v']], axis=2)
    kv = rows if nsa_past is None else jnp.concatenate([nsa_past, rows], axis=1)
    S = kv.shape[1]
    pe, w1, w2 = lw['nsa_cmp_pe'], lw['nsa_cmp_w1'], lw['nsa_cmp_w2']
    ck = compress(kv[:, :, 0], pe[0], w1[0], w2[0])
    cv = compress(kv[:, :, 1], pe[1], w1[1], w2[1])
    n_cmp = ck.shape[1]
    cmp_end = (jnp.arange(n_cmp) + 1) * NSA_BLOCK - 1
    n_blk = -(-S // NSA_BLOCK)
    sel = jnp.pad(kv[:, :, 2:], ((0, 0), (0, n_blk * NSA_BLOCK - S), (0, 0), (0, 0)))
    sel = sel.reshape(B, n_blk, NSA_BLOCK, 2, NSA_DH)
    n_top = min(NSA_TOPK, n_blk)
    win = win_rows if win_past is None else jnp.concatenate([win_past, win_rows], axis=1)
    new_win = win[:, win.shape[1] - min(NSA_WINDOW, win.shape[1]):]
    win_pos0 = pos0 - (0 if win_past is None else win_past.shape[1])
    win = jnp.pad(win, ((0, 0), (NSA_WINDOW, 0), (0, 0), (0, 0)))
    band = NSA_WINDOW + math.gcd(T, Q_BLOCK) - 1
    slopes = alibi_slopes(NSA_HEADS)
    bidx = jnp.arange(n_blk)
    bsel = jnp.arange(B)[:, None, None]

    def block(qb, gb, pb):
        blk = pb.shape[0]
        t = pb[:, None]
        d_c = t - cmp_end[None, :]
        s_c = jnp.einsum('bqhd,bnd->bhqn', qb, ck).astype(jnp.float32) * NSA_SCALE - slopes[:, None, None] * d_c.astype(jnp.float32)
        p_c = masked_softmax(s_c, d_c >= 0)
        o_c = jnp.einsum('bhqn,bnd->bqhd', p_c.astype(cv.dtype), cv)
        imp = jnp.pad(jnp.sum(p_c, axis=1), ((0, 0), (0, 0), (0, n_blk - n_cmp)))
        cur = (pb // NSA_BLOCK)[:, None]
        forced = (bidx[None] == 0) | (bidx[None] == cur) | (bidx[None] == cur - 1)
        score = jnp.where(bidx[None] <= cur, jnp.where(forced, NSA_FORCED, imp), -NSA_FORCED)
        _, top = lax.top_k(score, n_top)
        g = sel[bsel, top].reshape(B, blk, n_top * NSA_BLOCK, 2, NSA_DH)
        tok = (top[..., None] * NSA_BLOCK + jnp.arange(NSA_BLOCK)).reshape(B, blk, n_top * NSA_BLOCK)
        d_s = t[None] - tok
        s_s = jnp.einsum('bqhd,bqkd->bhqk', qb, g[..., 0, :]).astype(jnp.float32) * NSA_SCALE - slopes[None, :, None, None] * d_s[:, None].astype(jnp.float32)
        p_s = masked_softmax(s_s, (d_s >= 0)[:, None])
        o_s = jnp.einsum('bhqk,bqkd->bqhd', p_s.astype(g.dtype), g[..., 1, :])
        wk = lax.dynamic_slice_in_dim(win, pb[0] - win_pos0 + 1, band, axis=1)
        wpos = pb[0] - NSA_WINDOW + 1 + jnp.arange(band)
        d_w = t - wpos[None]
        m_w = (d_w >= 0) & (d_w < NSA_WINDOW) & (wpos[None] >= win_pos0)
        s_w = jnp.einsum('bqhd,bkd->bhqk', qb, wk[:, :, 0]).astype(jnp.float32) * NSA_SCALE - slopes[:, None, None] * d_w.astype(jnp.float32)
        p_w = masked_softmax(s_w, m_w)
        o_w = jnp.einsum('bhqk,bkd->bqhd', p_w.astype(wk.dtype), wk[:, :, 1])
        return gb[..., 0:1] * o_c + gb[..., 1:2] * o_s + gb[..., 2:3] * o_w

    o = map_query_blocks(block, (q, gates), pos)
    return o.reshape(B, T, NSA_HEADS * NSA_DH), rows, new_win


def diff_mixer(parts, pos, lw, layer, diff_past):
    B, T = parts['diff_q'].shape[:2]
    q = parts['diff_q'].reshape(B, T, DIFF_HEADS, 2, DIFF_DH)
    rows = jnp.concatenate([parts['diff_k'], parts['diff_v']], axis=-1)
    kv = rows if diff_past is None else jnp.concatenate([diff_past, rows], axis=1)
    S = kv.shape[1]
    k = kv[..., :2 * DIFF_DH].reshape(B, S, 2, DIFF_DH)
    v = kv[..., 2 * DIFF_DH:]
    k_pos = jnp.arange(S)
    lam_init = 0.8 - 0.6 * math.exp(-0.3 * layer)
    lp = lw['diff_lambda'].astype(jnp.float32)
    lam = jnp.exp(jnp.dot(lp[0], lp[1])) - jnp.exp(jnp.dot(lp[2], lp[3])) + lam_init
    slopes = alibi_slopes(DIFF_HEADS)

    def block(qb, pb):
        dist = pb[:, None] - k_pos[None, :]
        s = jnp.einsum('bqhcd,bkcd->bhcqk', qb, k).astype(jnp.float32) * DIFF_SCALE - slopes[:, None, None, None] * dist.astype(jnp.float32)
        p = masked_softmax(s, dist >= 0)
        w = p[:, :, 0] - lam * p[:, :, 1]
        return jnp.einsum('bhqk,bkd->bqhd', w.astype(v.dtype), v)

    o = map_query_blocks(block, (q,), pos)
    o = rms_norm(o, lw['diff_subln']) * (1.0 - lam_init)
    return o.reshape(B, T, DIFF_HEADS * DIFF_DV), rows


def mla_mixer(parts, pos, lw, mla_past):
    B, T = parts['mla_cq'].shape[:2]
    cq = rms_norm(parts['mla_cq'], lw['mla_q_norm'])
    q = (cq @ lw['mla_w_uq']).reshape(B, T, MLA_HEADS, MLA_NOPE + MLA_ROPE)
    q_nope, q_rope = q[..., :MLA_NOPE], apply_rope(q[..., MLA_NOPE:], pos)
    latent = rms_norm(parts['mla_ckv'], lw['mla_kv_norm'])
    k_rope = apply_rope(parts['mla_kr'], pos)
    rows = jnp.concatenate([latent, k_rope], axis=-1)
    kv = rows if mla_past is None else jnp.concatenate([mla_past, rows], axis=1)
    lat, kr = kv[..., :MLA_KV_LORA], kv[..., MLA_KV_LORA:]
    k_pos = jnp.arange(kv.shape[1])
    q_lat = jnp.einsum('bthn,chn->bthc', q_nope, lw['mla_w_uk'])

    def block(qlb, qrb, pb):
        dist = pb[:, None] - k_pos[None, :]
        s = (jnp.einsum('bqhc,bkc->bhqk', qlb, lat) + jnp.einsum('bqhr,bkr->bhqk', qrb, kr)).astype(jnp.float32) * MLA_SCALE
        p = masked_softmax(s, dist >= 0)
        return jnp.einsum('bhqk,bkc->bqhc', p.astype(lat.dtype), lat)

    o_lat = map_query_blocks(block, (q_lat, q_rope), pos)
    o = jnp.einsum('bthc,chv->bthv', o_lat, lw['mla_w_uv'])
    return o.reshape(B, T, MLA_HEADS * MLA_DV), rows


def token_mixer(x, pos0, lw, layer, past):
    T = x.shape[1]
    pos = pos0 + jnp.arange(T, dtype=jnp.int32)
    parts = split_in(x @ lw['w_in'])
    o_nsa, nsa_rows, win_state = nsa_mixer(
        parts, pos0, pos, lw,
        None if past is None else paged_rows(past['cache_nsa'], layer, past['page_table']),
        None if past is None else past['state_nsa_win'][layer])
    o_diff, diff_rows = diff_mixer(
        parts, pos, lw, layer,
        None if past is None else paged_rows(past['cache_diff'], layer, past['page_table']))
    o_mla, mla_rows = mla_mixer(
        parts, pos, lw,
        None if past is None else paged_rows(past['cache_mla'], layer, past['page_table']))
    y = jnp.concatenate([o_nsa, o_diff, o_mla], axis=-1) @ lw['w_out']
    return y, (nsa_rows, diff_rows, mla_rows, win_state)


def hier_moe(x, lw):
    B, T, D = x.shape
    t = x.reshape(B * T, D)
    p_grp = jax.nn.softmax((t @ lw['moe_w_group']).astype(jnp.float32) + lw['moe_b_group'], axis=-1)
    g_w, g_idx = lax.top_k(p_grp, 1)
    le = ((t @ lw['moe_w_expert']).astype(jnp.float32) + lw['moe_b_expert']).reshape(-1, N_GROUPS, EXP_PER_GROUP)
    le = jnp.take_along_axis(le, g_idx[:, :, None], axis=1)[:, 0]
    e_w, e_idx = lax.top_k(jax.nn.softmax(le, axis=-1), TOPK_IN_GROUP)
    w = g_w * e_w / jnp.sum(e_w, axis=-1, keepdims=True)
    gate = jnp.sum(jax.nn.one_hot(g_idx * EXP_PER_GROUP + e_idx, N_EXPERTS, dtype=jnp.float32) * w[..., None], axis=1)
    h = jax.nn.silu(jnp.einsum('nd,edh->neh', t, lw['moe_w1'])) * jnp.einsum('nd,edh->neh', t, lw['moe_w3'])
    y = jnp.einsum('neh,ehd->nd', h * gate[..., None].astype(h.dtype), lw['moe_w2'])
    return y.reshape(B, T, D)


def layer_step(x, pos0, lw, layer, past):
    mix, rows = token_mixer(x, pos0, lw, layer, past)
    x = layer_norm(DEEPNORM_ALPHA * x + mix, lw['ln1_g'], lw['ln1_b'])
    x = layer_norm(DEEPNORM_ALPHA * x + hier_moe(x, lw), lw['ln2_g'], lw['ln2_b'])
    return x, rows


def setup_inputs(seed: int = 0) -> dict:
    key = jax.random.key(seed)
    ks = iter(jax.random.split(key, 64))

    def nrm(shape, scale):
        return jax.random.normal(next(ks), shape, jnp.float32) * scale

    n_pages = PAST_LEN // PAGE_SIZE
    n_used = DEC_BATCH * n_pages
    n_pool = n_used + max(1, n_used // 4)
    win_buf = min(NSA_WINDOW, PAST_LEN)
    perm = jax.random.permutation(next(ks), n_pool)
    page_table = perm[:n_used].reshape(DEC_BATCH, n_pages).astype(jnp.int32)
    return {
        'x_prompt': nrm((BATCH, SEQ, D_MODEL), 1.0),
        'x_sample': nrm((DEC_BATCH, DEC_SEQ, D_MODEL), 1.0),
        'cache_nsa': nrm((DEPTH, n_pool, PAGE_SIZE, 4, NSA_DH), 1.0),
        'cache_diff': nrm((DEPTH, n_pool, PAGE_SIZE, DIFF_KV), 1.0),
        'cache_mla': nrm((DEPTH, n_pool, PAGE_SIZE, MLA_KV), 1.0),
        'state_nsa_win': nrm((DEPTH, DEC_BATCH, win_buf, 2, NSA_DH), 1.0),
        'page_table': page_table,
        'w_in': nrm((DEPTH, D_MODEL, IN_COLS), D_MODEL ** -0.5),
        'nsa_cmp_pe': nrm((DEPTH, 2, NSA_BLOCK, NSA_DH), 0.1),
        'nsa_cmp_w1': nrm((DEPTH, 2, NSA_BLOCK * NSA_DH, NSA_CMP_HID), (NSA_BLOCK * NSA_DH) ** -0.5),
        'nsa_cmp_w2': nrm((DEPTH, 2, NSA_CMP_HID, NSA_DH), NSA_CMP_HID ** -0.5),
        'diff_lambda': nrm((DEPTH, 4, DIFF_DH), 0.1),
        'diff_subln': 1.0 + nrm((DEPTH, DIFF_DV), 0.02),
        'mla_q_norm': 1.0 + nrm((DEPTH, MLA_Q_LORA), 0.02),
        'mla_w_uq': nrm((DEPTH, MLA_Q_LORA, MLA_HEADS * (MLA_NOPE + MLA_ROPE)), MLA_Q_LORA ** -0.5),
        'mla_kv_norm': 1.0 + nrm((DEPTH, MLA_KV_LORA), 0.02),
        'mla_w_uk': nrm((DEPTH, MLA_KV_LORA, MLA_HEADS, MLA_NOPE), MLA_KV_LORA ** -0.5),
        'mla_w_uv': nrm((DEPTH, MLA_KV_LORA, MLA_HEADS, MLA_DV), MLA_KV_LORA ** -0.5),
        'w_out': nrm((DEPTH, MIX_WIDTH, D_MODEL), MIX_WIDTH ** -0.5 * DEEPNORM_BETA),
        'ln1_g': 1.0 + nrm((DEPTH, D_MODEL), 0.02),
        'ln1_b': nrm((DEPTH, D_MODEL), 0.02),
        'moe_w_group': nrm((DEPTH, D_MODEL, N_GROUPS), D_MODEL ** -0.5),
        'moe_b_group': nrm((DEPTH, N_GROUPS), 0.01),
        'moe_w_expert': nrm((DEPTH, D_MODEL, N_EXPERTS), D_MODEL ** -0.5),
        'moe_b_expert': nrm((DEPTH, N_EXPERTS), 0.01),
        'moe_w1': nrm((DEPTH, N_EXPERTS, D_MODEL, EXP_HID), D_MODEL ** -0.5),
        'moe_w3': nrm((DEPTH, N_EXPERTS, D_MODEL, EXP_HID), D_MODEL ** -0.5),
        'moe_w2': nrm((DEPTH, N_EXPERTS, EXP_HID, D_MODEL), EXP_HID ** -0.5 * DEEPNORM_BETA),
        'ln2_g': 1.0 + nrm((DEPTH, D_MODEL), 0.02),
        'ln2_b': nrm((DEPTH, D_MODEL), 0.02),
    }


def reference(x_prompt, x_sample, cache_nsa, cache_diff, cache_mla, state_nsa_win, page_table,
              w_in, nsa_cmp_pe, nsa_cmp_w1, nsa_cmp_w2, diff_lambda, diff_subln,
              mla_q_norm, mla_w_uq, mla_kv_norm, mla_w_uk, mla_w_uv, w_out, ln1_g, ln1_b,
              moe_w_group, moe_b_group, moe_w_expert, moe_b_expert, moe_w1, moe_w3, moe_w2,
              ln2_g, ln2_b):
    past = {'cache_nsa': cache_nsa, 'cache_diff': cache_diff, 'cache_mla': cache_mla,
            'state_nsa_win': state_nsa_win, 'page_table': page_table}
    xp, xs = x_prompt, x_sample
    rows_p, rows_s = [], []
    for l in range(DEPTH):
        lw = {'w_in': w_in[l], 'nsa_cmp_pe': nsa_cmp_pe[l], 'nsa_cmp_w1': nsa_cmp_w1[l],
              'nsa_cmp_w2': nsa_cmp_w2[l], 'diff_lambda': diff_lambda[l], 'diff_subln': diff_subln[l],
              'mla_q_norm': mla_q_norm[l], 'mla_w_uq': mla_w_uq[l], 'mla_kv_norm': mla_kv_norm[l],
              'mla_w_uk': mla_w_uk[l], 'mla_w_uv': mla_w_uv[l], 'w_out': w_out[l],
              'ln1_g': ln1_g[l], 'ln1_b': ln1_b[l], 'moe_w_group': moe_w_group[l],
              'moe_b_group': moe_b_group[l], 'moe_w_expert': moe_w_expert[l],
              'moe_b_expert': moe_b_expert[l], 'moe_w1': moe_w1[l], 'moe_w3': moe_w3[l],
              'moe_w2': moe_w2[l], 'ln2_g': ln2_g[l], 'ln2_b': ln2_b[l]}
        xp, rp = layer_step(xp, 0, lw, l, None)
        xs, rs = layer_step(xs, PAST_LEN, lw, l, past)
        rows_p.append(rp)
        rows_s.append(rs)
    new_nsa_prompt = jnp.stack([r[0] for r in rows_p])
    new_nsa_sample = jnp.stack([r[0] for r in rows_s])
    new_diff_prompt = jnp.stack([r[1] for r in rows_p])
    new_diff_sample = jnp.stack([r[1] for r in rows_s])
    new_mla_prompt = jnp.stack([r[2] for r in rows_p])
    new_mla_sample = jnp.stack([r[2] for r in rows_s])
    new_win_prompt = jnp.stack([r[3] for r in rows_p])
    new_win_sample = jnp.stack([r[3] for r in rows_s])
    return (xp, xs, new_nsa_prompt, new_nsa_sample, new_diff_prompt, new_diff_sample,
            new_mla_prompt, new_mla_sample, new_win_prompt, new_win_sample)
```

```python
import functools
import math

import numpy as np
import jax
import jax.numpy as jnp
from jax import lax
from jax.experimental import pallas as pl
from jax.experimental.pallas import tpu as pltpu

F32 = jnp.float32
BF16 = jnp.bfloat16
I32 = jnp.int32

NSA_DH = 64
NSA_HEADS = 8
NSA_BLOCK = 64
NSA_BLOCK_SHIFT = 6
NSA_TOPK = 16
NSA_WINDOW = 512
NSA_CMP_HID = 128
NSA_FORCED = 1e4
NSA_SCALE = NSA_DH ** -0.5
DIFF_DH = 64
DIFF_DV = 128
DIFF_HEADS = 4
DIFF_SCALE = DIFF_DH ** -0.5
MLA_DV = 128
MLA_HEADS = 8
MLA_Q_LORA = 384
MLA_KV_LORA = 128
MLA_NOPE = 64
MLA_ROPE = 32
MLA_SLOT = 256
MLA_SCALE = (MLA_NOPE + MLA_ROPE) ** -0.5
ROPE_BASE = 10000.0
N_GROUPS = 4
EXP_PER_GROUP = 8
N_EXPERTS = 32
LN_EPS = 1e-5
RMS_EPS = 1e-6
NEG_INF = -1e30
LANES = 128
PAGE = 128
MIB = 2 ** 20

C_NSA_Q = 0
C_NSA_ROWS = 512
C_WIN = 768
C_DIFF_Q = 896
C_DIFF_KV = 1408
C_MLA_CQ = 1664
C_MLA_CKV = 2048
C_MLA_KR = 2176
C_MLA_KRS = 2208
C_GATE = 2240
H_COLS = 2304


def _alibi(n):
    return [2.0 ** (-8.0 * (i + 1) / n) for i in range(n)]


def _dot(a, b):
    return jnp.dot(a, b, preferred_element_type=F32)


def _dot_nt(a, b):
    return lax.dot_general(a, b, (((1,), (1,)), ((), ())), preferred_element_type=F32)


def _iota(shape, dim):
    return lax.broadcasted_iota(I32, shape, dim)


def _masked_softmax(s, mask):
    s = jnp.where(mask, s, NEG_INF)
    m = jnp.max(s, axis=-1, keepdims=True)
    e = jnp.where(mask, jnp.exp(s - m), 0.0)
    d = jnp.maximum(jnp.sum(e, axis=-1, keepdims=True), 1e-30)
    return e * (1.0 / d)


def _layer_norm(v, g, b):
    mu = jnp.mean(v, axis=-1, keepdims=True)
    c = v - mu
    var = jnp.mean(c * c, axis=-1, keepdims=True)
    return c * lax.rsqrt(var + LN_EPS) * g + b


def _rms(v, g):
    return v * lax.rsqrt(jnp.mean(v * v, axis=-1, keepdims=True) + RMS_EPS) * g


def _tile(n, pref):
    for t in range(min(n, pref), 7, -1):
        if n % t == 0 and t % 8 == 0:
            return t
    return n


def _params(sem, vmem_mib):
    return pltpu.CompilerParams(dimension_semantics=sem, vmem_limit_bytes=vmem_mib * MIB)


def _mm_kernel(a_ref, b_ref, *rest, has_abias, has_obias, act):
    rest = list(rest)
    a = a_ref[...]
    if has_abias:
        a = a + rest.pop(0)[...]
    acc = _dot(a.astype(BF16), b_ref[...].astype(BF16))
    if has_obias:
        acc = acc + rest.pop(0)[...]
    if act == "gelu":
        acc = jax.nn.gelu(acc)
    rest[0][...] = acc


def _mm(a, b, tm, tn, a_bias=None, o_bias=None, act=None, vmem_mib=48):
    M, K = a.shape
    N = b.shape[1]
    tm, tn = _tile(M, tm), min(tn, N)
    assert M % tm == 0 and N % tn == 0
    in_specs = [pl.BlockSpec((tm, K), lambda i, j: (i, 0)), pl.BlockSpec((K, tn), lambda i, j: (0, j))]
    args = [a, b]
    if a_bias is not None:
        in_specs.append(pl.BlockSpec((1, K), lambda i, j: (0, 0)))
        args.append(a_bias)
    if o_bias is not None:
        in_specs.append(pl.BlockSpec((1, tn), lambda i, j: (0, j)))
        args.append(o_bias)
    return pl.pallas_call(
        functools.partial(_mm_kernel, has_abias=a_bias is not None, has_obias=o_bias is not None, act=act),
        grid=(M // tm, N // tn),
        in_specs=in_specs,
        out_specs=pl.BlockSpec((tm, tn), lambda i, j: (i, j)),
        out_shape=jax.ShapeDtypeStruct((M, N), F32),
        compiler_params=_params(("parallel", "parallel"), vmem_mib),
        name="mm",
    )(*args)


def _prep_kernel(h_ref, c_ref, s_ref, qn_ref, kvn_ref, cqn_ref, lat_ref, kr_ref, gate_ref):
    cq = h_ref[:, C_MLA_CQ:C_MLA_CKV]
    cqn_ref[...] = _rms(cq, qn_ref[...])
    ckv = h_ref[:, C_MLA_CKV:C_MLA_KR]
    lat_ref[...] = _rms(ckv, kvn_ref[...])
    kr = h_ref[:, C_MLA_KR:C_MLA_KRS]
    krs = h_ref[:, C_MLA_KRS:C_GATE]
    kr_ref[...] = kr * c_ref[...] + krs * s_ref[...]
    gate_ref[...] = jax.nn.sigmoid(h_ref[:, C_GATE:C_GATE + NSA_HEADS * 3])


def _prep(h, cos32, sin32, q_norm, kv_norm, tm=512):
    N = h.shape[0]
    tm = _tile(N, tm)
    row = lambda w: pl.BlockSpec((tm, w), lambda i: (i, 0))
    const = lambda w: pl.BlockSpec((1, w), lambda i: (0, 0))
    return pl.pallas_call(
        _prep_kernel,
        grid=(N // tm,),
        in_specs=[row(H_COLS), row(MLA_ROPE), row(MLA_ROPE), const(MLA_Q_LORA), const(MLA_KV_LORA)],
        out_specs=[row(MLA_Q_LORA), row(MLA_KV_LORA), row(MLA_ROPE), row(NSA_HEADS * 3)],
        out_shape=[jax.ShapeDtypeStruct((N, MLA_Q_LORA), F32), jax.ShapeDtypeStruct((N, MLA_KV_LORA), F32),
                   jax.ShapeDtypeStruct((N, MLA_ROPE), F32), jax.ShapeDtypeStruct((N, NSA_HEADS * 3), F32)],
        compiler_params=_params(("parallel",), 40),
        name="prep",
    )(h, cos32, sin32, q_norm, kv_norm)


def _mla_q_kernel(q_ref, c_ref, s_ref, w_ref, o_ref):
    nope = q_ref[:, 0:512]
    rope = q_ref[:, 512:768] * c_ref[...] + q_ref[:, 768:1024] * s_ref[...]
    a = jnp.concatenate([nope, rope], axis=-1).astype(BF16)
    o_ref[...] = _dot(a, w_ref[...])


def _mla_q(q1, cos256, sin256, w_abs, tm=512):
    N = q1.shape[0]
    tm = _tile(N, tm)
    row = lambda w: pl.BlockSpec((tm, w), lambda i: (i, 0))
    return pl.pallas_call(
        _mla_q_kernel,
        grid=(N // tm,),
        in_specs=[row(1024), row(256), row(256), pl.BlockSpec(w_abs.shape, lambda i: (0, 0))],
        out_specs=row(MLA_HEADS * MLA_SLOT),
        out_shape=jax.ShapeDtypeStruct((N, MLA_HEADS * MLA_SLOT), F32),
        compiler_params=_params(("parallel",), 40),
        name="mla_q",
    )(q1, cos256, sin256, w_abs)


def _top_blocks(score, jb, n_top):
    sel = jnp.zeros(score.shape, jnp.bool_)
    picks = []
    for _ in range(n_top):
        m = jnp.max(score, axis=-1, keepdims=True)
        idx = jnp.min(jnp.where(score == m, jb, 1 << 30), axis=-1, keepdims=True)
        hit = jb == idx
        sel = jnp.logical_or(sel, hit)
        score = jnp.where(hit, -3e38, score)
        picks.append(idx)
    return sel, picks


def _pnsa_kernel(q_ref, g_ref, ck_ref, cv_ref, sk_ref, sv_ref, wk_ref, wv_ref, o_ref, *, tq, T, n_cmp, n_blk):
    q0 = pl.program_id(1) * tq
    t = q0 + _iota((tq, 1), 0)
    slopes = _alibi(NSA_HEADS)
    ck = ck_ref[0].astype(BF16)
    cv = cv_ref[0].astype(BF16)
    jc = _iota((1, n_cmp), 1)
    d_c = t - ((jc + 1) * NSA_BLOCK - 1)
    m_c = d_c >= 0
    d_cf = d_c.astype(F32)
    imp = jnp.zeros((tq, n_cmp), F32)
    o_c = []
    for h in range(NSA_HEADS):
        qh = q_ref[0, h].astype(BF16)
        p = _masked_softmax(_dot_nt(qh, ck) * NSA_SCALE - slopes[h] * d_cf, m_c)
        imp = imp + p
        o_c.append(_dot(p.astype(BF16), cv))
    if n_blk > n_cmp:
        imp = jnp.concatenate([imp, jnp.zeros((tq, n_blk - n_cmp), F32)], axis=-1)
    jb = _iota((1, n_blk), 1)
    cur = t >> NSA_BLOCK_SHIFT
    forced = (jb == 0) | (jb == cur) | (jb == cur - 1)
    score = jnp.where(jb <= cur, jnp.where(forced, NSA_FORCED, imp), -NSA_FORCED)
    sel, _ = _top_blocks(score, jb, min(NSA_TOPK, n_blk))
    expand = ((_iota((n_blk, T), 1) >> NSA_BLOCK_SHIFT) == _iota((n_blk, T), 0))
    key_sel = _dot(jnp.where(sel, 1.0, 0.0).astype(BF16), jnp.where(expand, 1.0, 0.0).astype(BF16)) > 0.5
    d_s = t - _iota((1, T), 1)
    m_s = key_sel & (d_s >= 0)
    d_sf = d_s.astype(F32)
    sk = sk_ref[0].astype(BF16)
    sv = sv_ref[0].astype(BF16)
    band = NSA_WINDOW + tq
    wk = wk_ref[0, pl.ds(pl.multiple_of(q0, tq), band), :].astype(BF16)
    wv = wv_ref[0, pl.ds(pl.multiple_of(q0, tq), band), :].astype(BF16)
    wpos = q0 - NSA_WINDOW + _iota((1, band), 1)
    d_w = t - wpos
    m_w = (d_w >= 0) & (d_w < NSA_WINDOW) & (wpos >= 0)
    d_wf = d_w.astype(F32)
    for h in range(NSA_HEADS):
        qh = q_ref[0, h].astype(BF16)
        p_s = _masked_softmax(_dot_nt(qh, sk) * NSA_SCALE - slopes[h] * d_sf, m_s)
        o_s = _dot(p_s.astype(BF16), sv)
        p_w = _masked_softmax(_dot_nt(qh, wk) * NSA_SCALE - slopes[h] * d_wf, m_w)
        o_w = _dot(p_w.astype(BF16), wv)
        g = g_ref[0, h]
        o_ref[0, h] = g[:, 0:1] * o_c[h] + g[:, 1:2] * o_s + g[:, 2:3] * o_w


def _prompt_nsa(q4, g4, ck, cv, sk, sv, wkp, wvp, tq=128):
    B, H, T, dh = q4.shape
    n_cmp = ck.shape[1]
    n_blk = -(-T // NSA_BLOCK)
    assert T % tq == 0 and T % NSA_BLOCK == 0
    per_b = lambda shape: pl.BlockSpec((1,) + shape, lambda b, i: (b,) + (0,) * len(shape))
    return pl.pallas_call(
        functools.partial(_pnsa_kernel, tq=tq, T=T, n_cmp=n_cmp, n_blk=n_blk),
        grid=(B, T // tq),
        in_specs=[pl.BlockSpec((1, H, tq, dh), lambda b, i: (b, 0, i, 0)),
                  pl.BlockSpec((1, H, tq, 3), lambda b, i: (b, 0, i, 0)),
                  per_b((n_cmp, dh)), per_b((n_cmp, dh)), per_b((T, dh)), per_b((T, dh)),
                  per_b((T + NSA_WINDOW, dh)), per_b((T + NSA_WINDOW, dh))],
        out_specs=pl.BlockSpec((1, H, tq, dh), lambda b, i: (b, 0, i, 0)),
        out_shape=jax.ShapeDtypeStruct((B, H, T, dh), F32),
        compiler_params=_params(("parallel", "parallel"), 48),
        name="prompt_nsa",
    )(q4, g4, ck, cv, sk, sv, wkp, wvp)


def _diff_lambda(lp_ref):
    lp = lp_ref[...]
    a = jnp.sum(lp[0:1] * lp[1:2], axis=-1, keepdims=True)
    b = jnp.sum(lp[2:3] * lp[3:4], axis=-1, keepdims=True)
    return jnp.exp(a) - jnp.exp(b)


def _pdiff_kernel(q_ref, k1_ref, k2_ref, v_ref, lp_ref, g_ref, o_ref, *, tq, T, lam_init):
    q0 = pl.program_id(1) * tq
    dist = (q0 + _iota((tq, 1), 0)) - _iota((1, T), 1)
    mask = dist >= 0
    distf = dist.astype(F32)
    lam = _diff_lambda(lp_ref) + lam_init
    ks = (k1_ref[0].astype(BF16), k2_ref[0].astype(BF16))
    v = v_ref[0].astype(BF16)
    slopes = _alibi(DIFF_HEADS)
    for h in range(DIFF_HEADS):
        ps = []
        for c in range(2):
            qh = q_ref[0, 2 * h + c].astype(BF16)
            ps.append(_masked_softmax(_dot_nt(qh, ks[c]) * DIFF_SCALE - slopes[h] * distf, mask))
        w = ps[0] - lam * ps[1]
        o = _dot(w.astype(BF16), v)
        o_ref[0, :, h * DIFF_DV:(h + 1) * DIFF_DV] = _rms(o, g_ref[...]) * (1.0 - lam_init)


def _prompt_diff(q5, k1, k2, v, lam_p, subln, lam_init, tq=128):
    B, HC, T, dh = q5.shape
    per_b = lambda shape: pl.BlockSpec((1,) + shape, lambda b, i: (b,) + (0,) * len(shape))
    return pl.pallas_call(
        functools.partial(_pdiff_kernel, tq=tq, T=T, lam_init=lam_init),
        grid=(B, T // tq),
        in_specs=[pl.BlockSpec((1, HC, tq, dh), lambda b, i: (b, 0, i, 0)),
                  per_b((T, dh)), per_b((T, dh)), per_b((T, DIFF_DV)),
                  pl.BlockSpec((4, DIFF_DH), lambda b, i: (0, 0)),
                  pl.BlockSpec((1, DIFF_DV), lambda b, i: (0, 0))],
        out_specs=pl.BlockSpec((1, tq, DIFF_HEADS * DIFF_DV), lambda b, i: (b, i, 0)),
        out_shape=jax.ShapeDtypeStruct((B, T, DIFF_HEADS * DIFF_DV), F32),
        compiler_params=_params(("parallel", "parallel"), 48),
        name="prompt_diff",
    )(q5, k1, k2, v, lam_p, subln)


def _pmla_kernel(q_ref, k_ref, o_ref, *, tq, T):
    q0 = pl.program_id(1) * tq
    mask = ((q0 + _iota((tq, 1), 0)) - _iota((1, T), 1)) >= 0
    k = k_ref[0].astype(BF16)
    lat = k[:, 0:MLA_KV_LORA]
    for h in range(MLA_HEADS):
        qh = q_ref[0, :, h * MLA_SLOT:(h + 1) * MLA_SLOT].astype(BF16)
        p = _masked_softmax(_dot_nt(qh, k) * MLA_SCALE, mask)
        o_ref[0, :, h * MLA_KV_LORA:(h + 1) * MLA_KV_LORA] = _dot(p.astype(BF16), lat)


def _prompt_mla(q_cat, k_cat, tq=128):
    B, T, _ = q_cat.shape
    return pl.pallas_call(
        functools.partial(_pmla_kernel, tq=tq, T=T),
        grid=(B, T // tq),
        in_specs=[pl.BlockSpec((1, tq, MLA_HEADS * MLA_SLOT), lambda b, i: (b, i, 0)),
                  pl.BlockSpec((1, T, MLA_SLOT), lambda b, i: (b, 0, 0))],
        out_specs=pl.BlockSpec((1, tq, MLA_HEADS * MLA_KV_LORA), lambda b, i: (b, i, 0)),
        out_shape=jax.ShapeDtypeStruct((B, T, MLA_HEADS * MLA_KV_LORA), F32),
        compiler_params=_params(("parallel", "parallel"), 48),
        name="prompt_mla",
    )(q_cat, k_cat)


def _page_copies(pt_ref, cache_ref, buf_ref, sem_ref, layer, seq, chunk, slot, *, n_pages, ppc, width):
    copies = []
    for j in range(ppc):
        page = pt_ref[seq * n_pages + chunk * ppc + j]
        copies.append(pltpu.make_async_copy(
            cache_ref.at[layer, page],
            buf_ref.at[slot, pl.ds(j * PAGE, PAGE), :],
            sem_ref.at[slot]))
    return copies


def _paged_kernel(pt_ref, q_ref, new_ref, *rest, kind, layer, n_pages, ppc, past_len, n_tok, lam_init):
    if kind == "diff":
        lp_ref, g_ref, cache_ref, o_ref, buf_ref, sem_ref, m_ref, l_ref, acc_ref = rest
        kw, vlo = 128, 128
    else:
        cache_ref, o_ref, buf_ref, sem_ref, m_ref, l_ref, acc_ref = rest
        kw, vlo = buf_ref.shape[2], 0
    width = cache_ref.shape[3]
    b = pl.program_id(0)
    c = pl.program_id(1)
    nb = pl.num_programs(0)
    nc = pl.num_programs(1)
    step = b * nc + c
    slot = step % 2
    copies = functools.partial(_page_copies, pt_ref, cache_ref, buf_ref, sem_ref, layer,
                               n_pages=n_pages, ppc=ppc, width=width)

    @pl.when(step == 0)
    def _():
        for cp in copies(b, c, slot):
            cp.start()

    @pl.when(step + 1 < nb * nc)
    def _():
        nxt = step + 1
        for cp in copies(nxt // nc, nxt % nc, 1 - slot):
            cp.start()

    @pl.when(c == 0)
    def _():
        m_ref[...] = jnp.full(m_ref.shape, NEG_INF, F32)
        l_ref[...] = jnp.zeros(l_ref.shape, F32)
        acc_ref[...] = jnp.zeros(acc_ref.shape, F32)

    rows = q_ref.shape[1]
    q = q_ref[0].astype(BF16)
    r = _iota((rows, 1), 0)
    if kind == "diff":
        tok = (r % (n_tok * DIFF_HEADS)) // DIFF_HEADS
        head = r % DIFF_HEADS
        slope = jnp.zeros((rows, 1), F32)
        for h, sl in enumerate(_alibi(DIFF_HEADS)):
            slope = jnp.where(head == h, sl, slope)
        scale = DIFF_SCALE
    else:
        tok = r // MLA_HEADS
        slope = None
        scale = MLA_SCALE
    pos = past_len + tok

    def update(k, v, kpos, valid):
        s = _dot_nt(q[:, 0:kw], k) * scale
        dist = pos - kpos
        if slope is not None:
            s = s - slope * dist.astype(F32)
        mask = dist >= 0
        if valid is not None:
            mask = mask & valid
        s = jnp.where(mask, s, NEG_INF)
        m_old = m_ref[...]
        m_new = jnp.maximum(m_old, jnp.max(s, axis=-1, keepdims=True))
        alpha = jnp.exp(m_old - m_new)
        p = jnp.where(mask, jnp.exp(s - m_new), 0.0)
        l_ref[...] = alpha * l_ref[...] + jnp.sum(p, axis=-1, keepdims=True)
        acc_ref[...] = alpha * acc_ref[...] + _dot(p.astype(BF16), v)
        m_ref[...] = m_new

    for cp in copies(b, c, slot):
        cp.wait()
    keys = ppc * PAGE
    k = buf_ref[slot, :, 0:kw].astype(BF16)
    v = buf_ref[slot, :, vlo:vlo + 128].astype(BF16)
    update(k, v, c * keys + _iota((1, keys), 1), None)

    @pl.when(c == nc - 1)
    def _():
        nk = new_ref.shape[1]
        jn = _iota((1, nk), 1)
        kn = new_ref[0, :, 0:kw].astype(BF16)
        vn = new_ref[0, :, vlo:vlo + 128].astype(BF16)
        update(kn, vn, past_len + jn, jn < n_tok)
        o = acc_ref[...] * (1.0 / jnp.maximum(l_ref[...], 1e-30))
        if kind == "diff":
            half = rows // 2
            lam = _diff_lambda(lp_ref) + lam_init
            w = o[0:half] - lam * o[half:rows]
            o_ref[0] = _rms(w, g_ref[...]) * (1.0 - lam_init)
        else:
            o_ref[0] = o


def _paged_attention(kind, q, new_rows, cache, page_table, layer, extra=(), lam_init=0.0, ppc=32):
    B, rows, qw = q.shape
    n_pages = page_table.shape[1]
    ppc = min(ppc, n_pages)
    assert n_pages % ppc == 0 and cache.shape[2] == PAGE
    width = cache.shape[3]
    n_tok = rows // (2 * DIFF_HEADS if kind == "diff" else MLA_HEADS)
    out_rows = rows // 2 if kind == "diff" else rows
    nc = n_pages // ppc
    in_specs = [pl.BlockSpec((1, rows, qw), lambda b, c, pt: (b, 0, 0)),
                pl.BlockSpec((1,) + new_rows.shape[1:], lambda b, c, pt: (b, 0, 0))]
    in_specs += [pl.BlockSpec(e.shape, lambda b, c, pt: (0, 0)) for e in extra]
    in_specs.append(pl.BlockSpec(memory_space=pl.ANY))
    grid_spec = pltpu.PrefetchScalarGridSpec(
        num_scalar_prefetch=1,
        grid=(B, nc),
        in_specs=in_specs,
        out_specs=pl.BlockSpec((1, out_rows, 128), lambda b, c, pt: (b, 0, 0)),
        scratch_shapes=[pltpu.VMEM((2, ppc * PAGE, width), F32),
                        pltpu.SemaphoreType.DMA((2,)),
                        pltpu.VMEM((rows, 1), F32), pltpu.VMEM((rows, 1), F32), pltpu.VMEM((rows, 128), F32)],
    )
    return pl.pallas_call(
        functools.partial(_paged_kernel, kind=kind, layer=layer, n_pages=n_pages, ppc=ppc,
                          past_len=n_pages * PAGE, n_tok=n_tok, lam_init=lam_init),
        grid_spec=grid_spec,
        out_shape=jax.ShapeDtypeStruct((B, out_rows, 128), F32),
        compiler_params=_params(("arbitrary", "arbitrary"), 48),
        name="paged_" + kind,
    )(page_table.reshape(-1), q, new_rows, *extra, cache)


def _snsa_a_copies(pt_ref, cache_ref, buf_ref, sem_ref, layer, seq, slot, *, n_pages):
    copies = []
    for p in range(n_pages):
        page = pt_ref[seq * n_pages + p]
        copies.append(pltpu.make_async_copy(
            cache_ref.at[layer, page, :, :, 0:128],
            buf_ref.at[slot, :, :, p, :],
            sem_ref.at[slot]))
    return copies


def _snsa_a_kernel(pt_ref, q_ref, win_ref, w1_ref, b1_ref, w2_ref, cache_ref, oc_ref, ow_ref, top_ref,
                   buf_ref, sem_ref, *, layer, n_pages, past_len, n_tok):
    b = pl.program_id(0)
    nb = pl.num_programs(0)
    slot = b % 2
    copies = functools.partial(_snsa_a_copies, pt_ref, cache_ref, buf_ref, sem_ref, layer, n_pages=n_pages)

    @pl.when(b == 0)
    def _():
        for cp in copies(b, slot):
            cp.start()

    @pl.when(b + 1 < nb)
    def _():
        for cp in copies(b + 1, 1 - slot):
            cp.start()

    rows = q_ref.shape[1]
    q = q_ref[0].astype(BF16)
    r = _iota((rows, 1), 0)
    head = r % NSA_HEADS
    pos = past_len + r // NSA_HEADS
    slope = jnp.zeros((rows, 1), F32)
    for h, sl in enumerate(_alibi(NSA_HEADS)):
        slope = jnp.where(head == h, sl, slope)

    nw = win_ref.shape[1]
    wkv = win_ref[0].astype(BF16)
    wpos = past_len + n_tok - nw + _iota((1, nw), 1)
    d_w = pos - wpos
    m_w = (d_w >= 0) & (d_w < NSA_WINDOW)
    p_w = _masked_softmax(_dot_nt(q, wkv) * NSA_SCALE - slope * d_w.astype(F32), m_w)
    ow_ref[0] = _dot(p_w.astype(BF16), wkv)

    for cp in copies(b, slot):
        cp.wait()

    halves = []
    for i in range(2):
        acc = jnp.zeros((n_pages, 2 * NSA_CMP_HID), F32)
        for rr in range(NSA_BLOCK):
            acc = acc + _dot(buf_ref[slot, i, rr].astype(BF16), w1_ref[rr])
        halves.append(acc)
    hid = jax.nn.gelu(jnp.concatenate(halves, axis=0) + b1_ref[...])
    ckv = _dot(hid.astype(BF16), w2_ref[...]).astype(BF16)
    n_cmp = 2 * n_pages
    col = _iota((1, n_cmp), 1)
    jc = 2 * (col % n_pages) + col // n_pages
    d_c = pos - ((jc + 1) * NSA_BLOCK - 1)
    p_c = _masked_softmax(_dot_nt(q, ckv) * NSA_SCALE - slope * d_c.astype(F32), d_c >= 0)
    oc_ref[0] = _dot(p_c.astype(BF16), ckv)

    imp = jnp.sum(p_c.reshape(n_tok, NSA_HEADS, n_cmp), axis=1)
    n_blk = -(-(past_len + n_tok) // NSA_BLOCK)
    pad = (-(n_cmp + 1)) % LANES + 1
    imp = jnp.concatenate([imp, jnp.zeros((n_tok, pad), F32)], axis=-1)
    jb = jnp.concatenate([jc, n_cmp + _iota((1, pad), 1)], axis=-1)
    cur = (past_len + _iota((n_tok, 1), 0)) >> NSA_BLOCK_SHIFT
    forced = (jb == 0) | (jb == cur) | (jb == cur - 1)
    score = jnp.where(jb <= cur, jnp.where(forced, NSA_FORCED, imp), -NSA_FORCED)
    score = jnp.where(jb < n_blk, score, -3e38)
    _, picks = _top_blocks(score, jb, NSA_TOPK)
    lane = _iota((n_tok, LANES), 1)
    top = jnp.zeros((n_tok, LANES), I32)
    for k, idx in enumerate(picks):
        top = jnp.where(lane == k, idx, top)
    top_ref[0] = top


def _sample_nsa_a(q, win_cat, w1cat, b1cat, w2cat, cache5, page_table, layer, n_tok):
    B, rows, _ = q.shape
    n_pages = page_table.shape[1]
    nw = win_cat.shape[1]
    const = lambda shape: pl.BlockSpec(shape, lambda b, pt: (0,) * len(shape))
    grid_spec = pltpu.PrefetchScalarGridSpec(
        num_scalar_prefetch=1,
        grid=(B,),
        in_specs=[pl.BlockSpec((1, rows, 128), lambda b, pt: (b, 0, 0)),
                  pl.BlockSpec((1, nw, 128), lambda b, pt: (b, 0, 0)),
                  const(w1cat.shape), const(b1cat.shape), const(w2cat.shape),
                  pl.BlockSpec(memory_space=pl.ANY)],
        out_specs=[pl.BlockSpec((1, rows, 128), lambda b, pt: (b, 0, 0)),
                   pl.BlockSpec((1, rows, 128), lambda b, pt: (b, 0, 0)),
                   pl.BlockSpec((1, n_tok, LANES), lambda b, pt: (b, 0, 0))],
        scratch_shapes=[pltpu.VMEM((2, 2, NSA_BLOCK, n_pages, 128), F32), pltpu.SemaphoreType.DMA((2,))],
    )
    return pl.pallas_call(
        functools.partial(_snsa_a_kernel, layer=layer, n_pages=n_pages, past_len=n_pages * PAGE, n_tok=n_tok),
        grid_spec=grid_spec,
        out_shape=[jax.ShapeDtypeStruct((B, rows, 128), F32), jax.ShapeDtypeStruct((B, rows, 128), F32),
                   jax.ShapeDtypeStruct((B, n_tok, LANES), I32)],
        compiler_params=_params(("arbitrary",), 56),
        name="sample_nsa_a",
    )(page_table.reshape(-1), q, win_cat, w1cat, b1cat, w2cat, cache5)


def _snsa_b_copies(pt_ref, top_ref, cache_ref, buf_ref, sem_ref, layer, seq, slot, *, n_pages, n_tok):
    copies = []
    n_cmp = 2 * n_pages
    for t in range(n_tok):
        for k in range(NSA_TOPK):
            blk = top_ref[(seq * n_tok + t) * NSA_TOPK + k]
            blk = jnp.where(blk < n_cmp, blk, 0)
            page = pt_ref[seq * n_pages + (blk >> 1)]
            copies.append(pltpu.make_async_copy(
                cache_ref.at[layer, page, blk & 1, :, 128:256],
                buf_ref.at[slot, t, pl.ds(k * NSA_BLOCK, NSA_BLOCK), :],
                sem_ref.at[slot]))
    return copies


def _snsa_b_kernel(pt_ref, top_ref, q_ref, new_ref, oc_ref, ow_ref, g_ref, cache_ref, o_ref, buf_ref, sem_ref,
                   *, layer, n_pages, past_len, n_tok):
    b = pl.program_id(0)
    nb = pl.num_programs(0)
    slot = b % 2
    copies = functools.partial(_snsa_b_copies, pt_ref, top_ref, cache_ref, buf_ref, sem_ref, layer,
                               n_pages=n_pages, n_tok=n_tok)

    @pl.when(b == 0)
    def _():
        for cp in copies(b, slot):
            cp.start()

    @pl.when(b + 1 < nb)
    def _():
        for cp in copies(b + 1, 1 - slot):
            cp.start()

    for cp in copies(b, slot):
        cp.wait()

    n_cmp = 2 * n_pages
    keys = NSA_TOPK * NSA_BLOCK
    lane = _iota((1, keys), 1)
    slot_of = lane >> NSA_BLOCK_SHIFT
    within = lane & (NSA_BLOCK - 1)
    nk = new_ref.shape[1]
    jn = _iota((1, nk), 1)
    new_kv = new_ref[0].astype(BF16)
    slopes = jnp.zeros((NSA_HEADS, 1), F32)
    hrow = _iota((NSA_HEADS, 1), 0)
    for h, sl in enumerate(_alibi(NSA_HEADS)):
        slopes = jnp.where(hrow == h, sl, slopes)
    for t in range(n_tok):
        blk_of = jnp.zeros((1, keys), I32)
        has_new = jnp.int32(0)
        for k in range(NSA_TOPK):
            blk = top_ref[(b * n_tok + t) * NSA_TOPK + k]
            blk_of = jnp.where(slot_of == k, blk, blk_of)
            has_new = jnp.maximum(has_new, (blk == n_cmp).astype(I32))
        pos = past_len + t
        q = q_ref[0, t].astype(BF16)
        kv = buf_ref[slot, t].astype(BF16)
        d_g = pos - (blk_of * NSA_BLOCK + within)
        m_g = (d_g >= 0) & (blk_of < n_cmp)
        s_g = jnp.where(m_g, _dot_nt(q, kv) * NSA_SCALE - slopes * d_g.astype(F32), NEG_INF)
        d_n = pos - (past_len + jn)
        m_n = (d_n >= 0) & (jn < has_new * n_tok)
        s_n = jnp.where(m_n, _dot_nt(q, new_kv) * NSA_SCALE - slopes * d_n.astype(F32), NEG_INF)
        m = jnp.maximum(jnp.max(s_g, axis=-1, keepdims=True), jnp.max(s_n, axis=-1, keepdims=True))
        e_g = jnp.where(m_g, jnp.exp(s_g - m), 0.0)
        e_n = jnp.where(m_n, jnp.exp(s_n - m), 0.0)
        den = jnp.sum(e_g, axis=-1, keepdims=True) + jnp.sum(e_n, axis=-1, keepdims=True)
        o_s = (_dot(e_g.astype(BF16), kv) + _dot(e_n.astype(BF16), new_kv)) * (1.0 / jnp.maximum(den, 1e-30))
        g = g_ref[0, t]
        o_ref[0, t] = g[:, 0:1] * oc_ref[0, t] + g[:, 1:2] * o_s + g[:, 2:3] * ow_ref[0, t]


def _sample_nsa_b(q, new_sel, o_c, o_w, gates, top, cache5, page_table, layer):
    B, n_tok, H, _ = q.shape
    n_pages = page_table.shape[1]
    nk = new_sel.shape[1]
    per_b = lambda shape: pl.BlockSpec((1,) + shape, lambda b, pt, tp: (b,) + (0,) * len(shape))
    grid_spec = pltpu.PrefetchScalarGridSpec(
        num_scalar_prefetch=2,
        grid=(B,),
        in_specs=[per_b((n_tok, H, 128)), per_b((nk, 128)), per_b((n_tok, H, 128)), per_b((n_tok, H, 128)),
                  per_b((n_tok, H, 3)), pl.BlockSpec(memory_space=pl.ANY)],
        out_specs=per_b((n_tok, H, 128)),
        scratch_shapes=[pltpu.VMEM((2, n_tok, NSA_TOPK * NSA_BLOCK, 128), F32), pltpu.SemaphoreType.DMA((2,))],
    )
    return pl.pallas_call(
        functools.partial(_snsa_b_kernel, layer=layer, n_pages=n_pages, past_len=n_pages * PAGE, n_tok=n_tok),
        grid_spec=grid_spec,
        out_shape=jax.ShapeDtypeStruct((B, n_tok, H, 128), F32),
        compiler_params=_params(("arbitrary",), 48),
        name="sample_nsa_b",
    )(page_table.reshape(-1), top.reshape(-1), q, new_sel, o_c, o_w, gates, cache5)


def _out_ln_kernel(a_ref, w_ref, x_ref, g_ref, b_ref, y_ref, yb_ref, *, alpha):
    mix = _dot(a_ref[...].astype(BF16), w_ref[...])
    y = _layer_norm(alpha * x_ref[...] + mix, g_ref[...], b_ref[...])
    y_ref[...] = y
    yb_ref[...] = y.astype(BF16)


def _out_ln(a, w_bf16, x, g, b, alpha, tm=256):
    N, K = a.shape
    tm = _tile(N, tm)
    D = w_bf16.shape[1]
    row = lambda w: pl.BlockSpec((tm, w), lambda i: (i, 0))
    const = lambda shape: pl.BlockSpec(shape, lambda i: (0, 0))
    return pl.pallas_call(
        functools.partial(_out_ln_kernel, alpha=alpha),
        grid=(N // tm,),
        in_specs=[row(K), const((K, D)), row(D), const((1, D)), const((1, D))],
        out_specs=[row(D), row(D)],
        out_shape=[jax.ShapeDtypeStruct((N, D), F32), jax.ShapeDtypeStruct((N, D), BF16)],
        compiler_params=_params(("parallel",), 48),
        name="out_ln",
    )(a, w_bf16, x, g, b)


def _split_bf16(v):
    hi = v.astype(BF16)
    lo = (v - hi.astype(F32)).astype(BF16)
    return hi, lo


def _route_kernel(x_ref, w_ref, b_ref, gate_ref):
    xh, xl = _split_bf16(x_ref[...])
    wh, wl = _split_bf16(w_ref[...])
    logits = _dot(xh, wh) + (_dot(xh, wl) + _dot(xl, wh)) + b_ref[...]
    tm = logits.shape[0]
    lane = _iota((tm, LANES), 1)
    is_grp = (lane >= N_EXPERTS) & (lane < N_EXPERTS + N_GROUPS)
    lg = jnp.where(is_grp, logits, NEG_INF)
    mg = jnp.max(lg, axis=-1, keepdims=True)
    eg = jnp.where(is_grp, jnp.exp(lg - mg), 0.0)
    pg = eg / jnp.sum(eg, axis=-1, keepdims=True)
    g_w = jnp.max(pg, axis=-1, keepdims=True)
    g_idx = jnp.min(jnp.where(is_grp & (pg == g_w), lane, 1 << 30), axis=-1, keepdims=True) - N_EXPERTS
    in_grp = (lane >= g_idx * EXP_PER_GROUP) & (lane < (g_idx + 1) * EXP_PER_GROUP)
    le = jnp.where(in_grp, logits, NEG_INF)
    me = jnp.max(le, axis=-1, keepdims=True)
    ee = jnp.where(in_grp, jnp.exp(le - me), 0.0)
    pe = ee / jnp.sum(ee, axis=-1, keepdims=True)
    pe = jnp.where(in_grp, pe, -1.0)
    e1 = jnp.max(pe, axis=-1, keepdims=True)
    i1 = jnp.min(jnp.where(pe == e1, lane, 1 << 30), axis=-1, keepdims=True)
    pe2 = jnp.where(lane == i1, -1.0, pe)
    e2 = jnp.max(pe2, axis=-1, keepdims=True)
    i2 = jnp.min(jnp.where(pe2 == e2, lane, 1 << 30), axis=-1, keepdims=True)
    tot = e1 + e2
    gate_ref[...] = jnp.where(lane == i1, g_w * e1 / tot, jnp.where(lane == i2, g_w * e2 / tot, 0.0))


def _route(x, w_route, b_route, tm=512):
    N, D = x.shape
    tm = _tile(N, tm)
    return pl.pallas_call(
        _route_kernel,
        grid=(N // tm,),
        in_specs=[pl.BlockSpec((tm, D), lambda i: (i, 0)), pl.BlockSpec((D, LANES), lambda i: (0, 0)),
                  pl.BlockSpec((1, LANES), lambda i: (0, 0))],
        out_specs=pl.BlockSpec((tm, LANES), lambda i: (i, 0)),
        out_shape=jax.ShapeDtypeStruct((N, LANES), F32),
        compiler_params=_params(("parallel",), 40),
        name="route",
    )(x, w_route, b_route)


def _moe_kernel(x_ref, gate_ref, w1_ref, w3_ref, w2_ref, o_ref):
    e = pl.program_id(1)
    x = x_ref[...]
    h1 = _dot(x, w1_ref[0].astype(BF16))
    h3 = _dot(x, w3_ref[0].astype(BF16))
    lane = _iota(gate_ref.shape, 1)
    g = jnp.sum(jnp.where(lane == e, gate_ref[...], 0.0), axis=-1, keepdims=True)
    h = (jax.nn.silu(h1) * h3) * g
    y = _dot(h.astype(BF16), w2_ref[0].astype(BF16))

    @pl.when(e == 0)
    def _():
        o_ref[...] = y

    @pl.when(e > 0)
    def _():
        o_ref[...] += y


def _moe(xb, gate, w1, w3, w2, tm=512):
    N, D = xb.shape
    tm = _tile(N, tm)
    E, _, Hd = w1.shape
    return pl.pallas_call(
        _moe_kernel,
        grid=(N // tm, E),
        in_specs=[pl.BlockSpec((tm, D), lambda i, e: (i, 0)), pl.BlockSpec((tm, LANES), lambda i, e: (i, 0)),
                  pl.BlockSpec((1, D, Hd), lambda i, e: (e, 0, 0)), pl.BlockSpec((1, D, Hd), lambda i, e: (e, 0, 0)),
                  pl.BlockSpec((1, Hd, D), lambda i, e: (e, 0, 0))],
        out_specs=pl.BlockSpec((tm, D), lambda i, e: (i, 0)),
        out_shape=jax.ShapeDtypeStruct((N, D), F32),
        compiler_params=_params(("parallel", "arbitrary"), 56),
        name="moe",
    )(xb, gate, w1, w3, w2)


def _res_ln_kernel(x_ref, y_ref, g_ref, b_ref, o_ref, *, alpha):
    o_ref[...] = _layer_norm(alpha * x_ref[...] + y_ref[...], g_ref[...], b_ref[...])


def _res_ln(x, y, g, b, alpha, tm=512):
    N, D = x.shape
    tm = _tile(N, tm)
    row = pl.BlockSpec((tm, D), lambda i: (i, 0))
    const = pl.BlockSpec((1, D), lambda i: (0, 0))
    return pl.pallas_call(
        functools.partial(_res_ln_kernel, alpha=alpha),
        grid=(N // tm,),
        in_specs=[row, row, const, const],
        out_specs=row,
        out_shape=jax.ShapeDtypeStruct((N, D), F32),
        compiler_params=_params(("parallel",), 40),
        name="res_ln",
    )(x, y, g, b)


def _layer_weights(w_in, pe, w1, w2, w_uq, w_uk, w_uv, wg, bg, we, be):
    D = w_in.shape[0]
    o = np.cumsum([0, 512, 64, 64, 64, 64, 64, 64, 24, 512, 128, 128, 384, 128, 32])
    seg = lambda i: w_in[:, o[i]:o[i + 1]]
    kr = seg(13)
    half = MLA_ROPE // 2
    w_in_p = jnp.concatenate(
        [seg(0), seg(1), seg(2), seg(3), seg(4), seg(5), seg(6), seg(8), seg(9), seg(10), seg(11), seg(12), kr,
         jnp.concatenate([kr[:, half:], kr[:, :half]], axis=1), seg(7),
         jnp.zeros((D, H_COLS - C_GATE - 24), F32)], axis=1)
    w1k = w1[0].reshape(NSA_BLOCK, NSA_DH, NSA_CMP_HID)
    w1v = w1[1].reshape(NSA_BLOCK, NSA_DH, NSA_CMP_HID)
    z = jnp.zeros_like(w1k)
    w1cat = jnp.concatenate([jnp.concatenate([w1k, z], axis=2), jnp.concatenate([z, w1v], axis=2)], axis=1)
    z2 = jnp.zeros_like(w2[0])
    w2cat = jnp.concatenate([jnp.concatenate([w2[0], z2], axis=1), jnp.concatenate([z2, w2[1]], axis=1)], axis=0)
    uq = w_uq.reshape(MLA_Q_LORA, MLA_HEADS, MLA_NOPE + MLA_ROPE)
    uq_r = uq[:, :, MLA_NOPE:]
    uq_rs = jnp.concatenate([uq_r[:, :, half:], uq_r[:, :, :half]], axis=2)
    w_uq_p = jnp.concatenate([uq[:, :, :MLA_NOPE].reshape(MLA_Q_LORA, -1), uq_r.reshape(MLA_Q_LORA, -1),
                              uq_rs.reshape(MLA_Q_LORA, -1)], axis=1)
    w_abs = jnp.zeros((MLA_HEADS * (MLA_NOPE + MLA_ROPE), MLA_HEADS * MLA_SLOT), F32)
    eye = jnp.eye(MLA_ROPE, dtype=F32)
    for h in range(MLA_HEADS):
        w_abs = w_abs.at[h * MLA_NOPE:(h + 1) * MLA_NOPE, h * MLA_SLOT:h * MLA_SLOT + MLA_KV_LORA].set(w_uk[:, h, :].T)
        r0 = MLA_HEADS * MLA_NOPE + h * MLA_ROPE
        c0 = h * MLA_SLOT + MLA_KV_LORA
        w_abs = w_abs.at[r0:r0 + MLA_ROPE, c0:c0 + MLA_ROPE].set(eye)
    w_uv_bd = jnp.zeros((MLA_HEADS * MLA_KV_LORA, MLA_HEADS * MLA_DV), F32)
    for h in range(MLA_HEADS):
        w_uv_bd = w_uv_bd.at[h * MLA_KV_LORA:(h + 1) * MLA_KV_LORA, h * MLA_DV:(h + 1) * MLA_DV].set(w_uv[:, h, :])
    w_route = jnp.concatenate([we, wg, jnp.zeros((D, LANES - N_EXPERTS - N_GROUPS), F32)], axis=1)
    b_route = jnp.concatenate([be, bg, jnp.zeros((LANES - N_EXPERTS - N_GROUPS,), F32)])[None, :]
    return dict(w_in_p=w_in_p, w1cat=w1cat.astype(BF16), w2cat=w2cat.astype(BF16), w_uq_p=w_uq_p,
                w_abs=w_abs.astype(BF16), w_uv_bd=w_uv_bd, w_route=w_route, b_route=b_route)


def _rope_tables(pos):
    half = MLA_ROPE // 2
    inv = ROPE_BASE ** (-jnp.arange(half, dtype=F32) / half)
    ang = pos.astype(F32)[:, None] * inv[None, :]
    cos, sin = jnp.cos(ang), jnp.sin(ang)
    cos32 = jnp.concatenate([cos, cos], axis=1)
    sin32 = jnp.concatenate([-sin, sin], axis=1)
    return cos32, sin32, jnp.tile(cos32, (1, MLA_HEADS)), jnp.tile(sin32, (1, MLA_HEADS))


def _layer(x_all, tables, lw, wl, layer, depth, Bp, T, Bs, Ts, cache_nsa, cache_diff, cache_mla, win_state, page_table):
    cos32, sin32, cos256, sin256 = tables
    Np = Bp * T
    alpha = (2 * depth) ** 0.25
    lam_init = 0.8 - 0.6 * math.exp(-0.3 * layer)
    n_pages = page_table.shape[1]
    past_len = n_pages * PAGE

    h = _mm(x_all, wl["w_in_p"], 1024, 256)
    cqn, lat, krope, gates = _prep(h, cos32, sin32, lw["mla_q_norm"][None, :], lw["mla_kv_norm"][None, :])
    q1 = _mm(cqn, wl["w_uq_p"], 1024, 1024)
    q_cat = _mla_q(q1, cos256, sin256, wl["w_abs"])

    nsa_rows = h[:, C_NSA_ROWS:C_WIN]
    win_rows = h[:, C_WIN:C_DIFF_Q]
    diff_rows = h[:, C_DIFF_KV:C_MLA_CQ]
    mla_rows = jnp.concatenate([lat, krope], axis=1)

    hp = h[:Np].reshape(Bp, T, H_COLS)
    rows_p = nsa_rows[:Np].reshape(Bp, T, 4, NSA_DH)
    n_cmp = T // NSA_BLOCK
    pe = lw["nsa_cmp_pe"].reshape(2, 1, NSA_BLOCK * NSA_DH)
    cmp_p = []
    for i in range(2):
        xin = rows_p[:, :n_cmp * NSA_BLOCK, i].reshape(Bp * n_cmp, NSA_BLOCK * NSA_DH)
        hid = _mm(xin, lw["nsa_cmp_w1"][i], 128, 128, a_bias=pe[i], act="gelu")
        cmp_p.append(_mm(hid, lw["nsa_cmp_w2"][i], 128, NSA_DH).reshape(Bp, n_cmp, NSA_DH))
    q4 = hp[:, :, C_NSA_Q:C_NSA_ROWS].reshape(Bp, T, NSA_HEADS, NSA_DH).transpose(0, 2, 1, 3)
    g4 = gates[:Np].reshape(Bp, T, NSA_HEADS, 3).transpose(0, 2, 1, 3)
    win_p = win_rows[:Np].reshape(Bp, T, 2, NSA_DH)
    win_pad = jnp.pad(win_p, ((0, 0), (NSA_WINDOW, 0), (0, 0), (0, 0)))
    o_nsa_p = _prompt_nsa(q4, g4, cmp_p[0], cmp_p[1], rows_p[:, :, 2], rows_p[:, :, 3],
                          win_pad[:, :, 0], win_pad[:, :, 1])
    o_nsa_p = o_nsa_p.transpose(0, 2, 1, 3).reshape(Np, NSA_HEADS * NSA_DH)
    q5 = hp[:, :, C_DIFF_Q:C_DIFF_KV].reshape(Bp, T, 2 * DIFF_HEADS, DIFF_DH).transpose(0, 2, 1, 3)
    dkv_p = diff_rows[:Np].reshape(Bp, T, 2 * DIFF_DH + DIFF_DV)
    o_diff_p = _prompt_diff(q5, dkv_p[:, :, 0:DIFF_DH], dkv_p[:, :, DIFF_DH:2 * DIFF_DH], dkv_p[:, :, 2 * DIFF_DH:],
                            lw["diff_lambda"], lw["diff_subln"][None, :], lam_init).reshape(Np, -1)
    k_cat_p = jnp.pad(mla_rows[:Np], ((0, 0), (0, MLA_SLOT - mla_rows.shape[1]))).reshape(Bp, T, MLA_SLOT)
    o_lat_p = _prompt_mla(q_cat[:Np].reshape(Bp, T, -1), k_cat_p).reshape(Np, -1)

    hs = h[Np:].reshape(Bs, Ts, H_COLS)
    pad_new = lambda a: jnp.pad(a, ((0, 0), (0, PAGE - Ts), (0, 0)))
    qd = hs[:, :, C_DIFF_Q:C_DIFF_KV].reshape(Bs, Ts, DIFF_HEADS, 2, DIFF_DH).transpose(0, 3, 1, 2, 4)
    qd = qd.reshape(Bs, 2, Ts * DIFF_HEADS, DIFF_DH)
    zq = jnp.zeros_like(qd[:, 0])
    q_diff = jnp.concatenate([jnp.concatenate([qd[:, 0], zq], axis=-1), jnp.concatenate([zq, qd[:, 1]], axis=-1)], axis=1)
    o_diff_s = _paged_attention("diff", q_diff, pad_new(diff_rows[Np:].reshape(Bs, Ts, -1)), cache_diff, page_table,
                                layer, extra=(lw["diff_lambda"], lw["diff_subln"][None, :]), lam_init=lam_init)
    o_diff_s = o_diff_s.reshape(Bs * Ts, DIFF_HEADS * DIFF_DV)
    q_mla = q_cat[Np:].reshape(Bs, Ts * MLA_HEADS, MLA_SLOT)
    o_lat_s = _paged_attention("mla", q_mla, pad_new(mla_rows[Np:].reshape(Bs, Ts, -1)), cache_mla, page_table, layer)
    o_lat_s = o_lat_s.reshape(Bs * Ts, MLA_HEADS * MLA_KV_LORA)
    cache5 = cache_nsa.reshape(cache_nsa.shape[0], cache_nsa.shape[1], 2, NSA_BLOCK, 4 * NSA_DH)
    qn = hs[:, :, C_NSA_Q:C_NSA_ROWS].reshape(Bs, Ts, NSA_HEADS, NSA_DH)
    qn = jnp.concatenate([qn, jnp.zeros_like(qn)], axis=-1)
    win_new = win_rows[Np:].reshape(Bs, Ts, 2 * NSA_DH)
    win_cat = jnp.concatenate([win_state.reshape(Bs, -1, 2 * NSA_DH), win_new], axis=1)
    pe_bias = [_mm(jnp.broadcast_to(pe[i], (8, NSA_BLOCK * NSA_DH)), lw["nsa_cmp_w1"][i], 8, 128)[0:1] for i in range(2)]
    b1cat = jnp.concatenate(pe_bias, axis=1)
    o_c, o_w, top = _sample_nsa_a(qn.reshape(Bs, Ts * NSA_HEADS, 128), win_cat, wl["w1cat"], b1cat, wl["w2cat"],
                                  cache5, page_table, layer, Ts)
    top = top[:, :, :NSA_TOPK]
    new_sel = pad_new(nsa_rows[Np:].reshape(Bs, Ts, 4 * NSA_DH)[:, :, 2 * NSA_DH:])
    g_s = gates[Np:].reshape(Bs, Ts, NSA_HEADS, 3)
    o_nsa_s = _sample_nsa_b(qn, new_sel, o_c.reshape(Bs, Ts, NSA_HEADS, 128), o_w.reshape(Bs, Ts, NSA_HEADS, 128),
                            g_s, top, cache5, page_table, layer)
    o_nsa_s = o_nsa_s[..., NSA_DH:].reshape(Bs * Ts, NSA_HEADS * NSA_DH)

    o_lat = jnp.concatenate([o_lat_p, o_lat_s], axis=0)
    o_mla = _mm(o_lat, wl["w_uv_bd"], 1024, 1024)
    mix = jnp.concatenate([jnp.concatenate([o_nsa_p, o_nsa_s], axis=0),
                           jnp.concatenate([o_diff_p, o_diff_s], axis=0), o_mla], axis=1)
    x1, x1b = _out_ln(mix, lw["w_out"].astype(BF16), x_all, lw["ln1_g"][None, :], lw["ln1_b"][None, :], alpha)
    gate = _route(x1, wl["w_route"], wl["b_route"])
    y = _moe(x1b, gate, lw["moe_w1"], lw["moe_w3"], lw["moe_w2"])
    x2 = _res_ln(x1, y, lw["ln2_g"][None, :], lw["ln2_b"][None, :], alpha)

    new_win_p = win_p[:, T - min(NSA_WINDOW, T):]
    new_win_s = win_cat[:, win_cat.shape[1] - min(NSA_WINDOW, win_cat.shape[1]):].reshape(Bs, -1, 2, NSA_DH)
    rows = dict(
        nsa_p=rows_p, nsa_s=nsa_rows[Np:].reshape(Bs, Ts, 4, NSA_DH),
        diff_p=dkv_p, diff_s=diff_rows[Np:].reshape(Bs, Ts, -1),
        mla_p=mla_rows[:Np].reshape(Bp, T, -1), mla_s=mla_rows[Np:].reshape(Bs, Ts, -1),
        win_p=new_win_p, win_s=new_win_s)
    return x2, rows


def kernel(x_prompt, x_sample, cache_nsa, cache_diff, cache_mla, state_nsa_win, page_table, w_in, nsa_cmp_pe, nsa_cmp_w1, nsa_cmp_w2, diff_lambda, diff_subln, mla_q_norm, mla_w_uq, mla_kv_norm, mla_w_uk, mla_w_uv, w_out, ln1_g, ln1_b, moe_w_group, moe_b_group, moe_w_expert, moe_b_expert, moe_w1, moe_w3, moe_w2, ln2_g, ln2_b):
    Bp, T, D = x_prompt.shape
    Bs, Ts, _ = x_sample.shape
    depth = w_in.shape[0]
    past_len = page_table.shape[1] * PAGE
    pos = jnp.concatenate([jnp.tile(jnp.arange(T, dtype=I32), Bp),
                           jnp.tile(past_len + jnp.arange(Ts, dtype=I32), Bs)])
    tables = _rope_tables(pos)
    x_all = jnp.concatenate([x_prompt.reshape(Bp * T, D), x_sample.reshape(Bs * Ts, D)], axis=0)
    outs = []
    for l in range(depth):
        lw = dict(nsa_cmp_pe=nsa_cmp_pe[l], nsa_cmp_w1=nsa_cmp_w1[l], nsa_cmp_w2=nsa_cmp_w2[l],
                  diff_lambda=diff_lambda[l], diff_subln=diff_subln[l], mla_q_norm=mla_q_norm[l],
                  mla_kv_norm=mla_kv_norm[l], w_out=w_out[l], ln1_g=ln1_g[l], ln1_b=ln1_b[l],
                  moe_w1=moe_w1[l], moe_w3=moe_w3[l], moe_w2=moe_w2[l], ln2_g=ln2_g[l], ln2_b=ln2_b[l])
        wl = _layer_weights(w_in[l], nsa_cmp_pe[l], nsa_cmp_w1[l], nsa_cmp_w2[l], mla_w_uq[l], mla_w_uk[l],
                            mla_w_uv[l], moe_w_group[l], moe_b_group[l], moe_w_expert[l], moe_b_expert[l])
        x_all, rows = _layer(x_all, tables, lw, wl, l, depth, Bp, T, Bs, Ts, cache_nsa, cache_diff, cache_mla,
                             state_nsa_win[l], page_table)
        outs.append(rows)
    Np = Bp * T
    st = lambda k: jnp.stack([r[k] for r in outs])
    return (x_all[:Np].reshape(Bp, T, D), x_all[Np:].reshape(Bs, Ts, D),
            st("nsa_p"), st("nsa_s"), st("diff_p"), st("diff_s"), st("mla_p"), st("mla_s"), st("win_p"), st("win_s"))
```

```python
import functools
import math

import numpy as np
import jax
import jax.numpy as jnp
from jax import lax
from jax.experimental import pallas as pl
from jax.experimental.pallas import tpu as pltpu

F32 = jnp.float32
BF16 = jnp.bfloat16
I32 = jnp.int32

NSA_DH = 64
NSA_HEADS = 8
NSA_BLOCK = 64
NSA_BLOCK_SHIFT = 6
NSA_TOPK = 16
NSA_WINDOW = 512
NSA_CMP_HID = 128
NSA_FORCED = 1e4
NSA_SCALE = NSA_DH ** -0.5
DIFF_DH = 64
DIFF_DV = 128
DIFF_HEADS = 4
DIFF_SCALE = DIFF_DH ** -0.5
MLA_DV = 128
MLA_HEADS = 8
MLA_Q_LORA = 384
MLA_KV_LORA = 128
MLA_NOPE = 64
MLA_ROPE = 32
MLA_SLOT = 256
MLA_SCALE = (MLA_NOPE + MLA_ROPE) ** -0.5
ROPE_BASE = 10000.0
N_GROUPS = 4
EXP_PER_GROUP = 8
N_EXPERTS = 32
LN_EPS = 1e-5
RMS_EPS = 1e-6
NEG_INF = -1e30
LANES = 128
PAGE = 128
MIB = 2 ** 20

C_NSA_Q = 0
C_NSA_ROWS = 512
C_WIN = 768
C_DIFF_Q = 896
C_DIFF_KV = 1408
C_MLA_CQ = 1664
C_MLA_CKV = 2048
C_MLA_KR = 2176
C_MLA_KRS = 2208
C_GATE = 2240
H_COLS = 2304


def _alibi(n):
    return [2.0 ** (-8.0 * (i + 1) / n) for i in range(n)]


def _dot(a, b):
    return jnp.dot(a, b, preferred_element_type=F32)


def _dot_nt(a, b):
    return lax.dot_general(a, b, (((1,), (1,)), ((), ())), preferred_element_type=F32)


def _iota(shape, dim):
    return lax.broadcasted_iota(I32, shape, dim)


def _masked_softmax(s, mask):
    s = jnp.where(mask, s, NEG_INF)
    m = jnp.max(s, axis=-1, keepdims=True)
    e = jnp.where(mask, jnp.exp(s - m), 0.0)
    d = jnp.maximum(jnp.sum(e, axis=-1, keepdims=True), 1e-30)
    return e * (1.0 / d)


def _layer_norm(v, g, b):
    mu = jnp.mean(v, axis=-1, keepdims=True)
    c = v - mu
    var = jnp.mean(c * c, axis=-1, keepdims=True)
    return c * lax.rsqrt(var + LN_EPS) * g + b


def _rms(v, g):
    return v * lax.rsqrt(jnp.mean(v * v, axis=-1, keepdims=True) + RMS_EPS) * g


def _tile(n, pref):
    for t in range(min(n, pref), 7, -1):
        if n % t == 0 and t % 8 == 0:
            return t
    return n


def _params(sem, vmem_mib):
    return pltpu.CompilerParams(dimension_semantics=sem, vmem_limit_bytes=vmem_mib * MIB)


def _mm_kernel(a_ref, b_ref, *rest, has_abias, has_obias, act):
    rest = list(rest)
    a = a_ref[...]
    if has_abias:
        a = a + rest.pop(0)[...]
    acc = _dot(a.astype(BF16), b_ref[...].astype(BF16))
    if has_obias:
        acc = acc + rest.pop(0)[...]
    if act == "gelu":
        acc = jax.nn.gelu(acc)
    rest[0][...] = acc


def _mm(a, b, tm, tn, a_bias=None, o_bias=None, act=None, vmem_mib=48):
    M, K = a.shape
    N = b.shape[1]
    tm, tn = _tile(M, tm), min(tn, N)
    assert M % tm == 0 and N % tn == 0
    in_specs = [pl.BlockSpec((tm, K), lambda i, j: (i, 0)), pl.BlockSpec((K, tn), lambda i, j: (0, j))]
    args = [a, b]
    if a_bias is not None:
        in_specs.append(pl.BlockSpec((1, K), lambda i, j: (0, 0)))
        args.append(a_bias)
    if o_bias is not None:
        in_specs.append(pl.BlockSpec((1, tn), lambda i, j: (0, j)))
        args.append(o_bias)
    return pl.pallas_call(
        functools.partial(_mm_kernel, has_abias=a_bias is not None, has_obias=o_bias is not None, act=act),
        grid=(M // tm, N // tn),
        in_specs=in_specs,
        out_specs=pl.BlockSpec((tm, tn), lambda i, j: (i, j)),
        out_shape=jax.ShapeDtypeStruct((M, N), F32),
        compiler_params=_params(("parallel", "parallel"), vmem_mib),
        name="mm",
    )(*args)


def _prep_kernel(h_ref, c_ref, s_ref, qn_ref, kvn_ref, cqn_ref, lat_ref, kr_ref, gate_ref):
    cq = h_ref[:, C_MLA_CQ:C_MLA_CKV]
    cqn_ref[...] = _rms(cq, qn_ref[...])
    ckv = h_ref[:, C_MLA_CKV:C_MLA_KR]
    lat_ref[...] = _rms(ckv, kvn_ref[...])
    kr = h_ref[:, C_MLA_KR:C_MLA_KRS]
    krs = h_ref[:, C_MLA_KRS:C_GATE]
    kr_ref[...] = kr * c_ref[...] + krs * s_ref[...]
    gate_ref[...] = jax.nn.sigmoid(h_ref[:, C_GATE:C_GATE + NSA_HEADS * 3])


def _prep(h, cos32, sin32, q_norm, kv_norm, tm=512):
    N = h.shape[0]
    tm = _tile(N, tm)
    row = lambda w: pl.BlockSpec((tm, w), lambda i: (i, 0))
    const = lambda w: pl.BlockSpec((1, w), lambda i: (0, 0))
    return pl.pallas_call(
        _prep_kernel,
        grid=(N // tm,),
        in_specs=[row(H_COLS), row(MLA_ROPE), row(MLA_ROPE), const(MLA_Q_LORA), const(MLA_KV_LORA)],
        out_specs=[row(MLA_Q_LORA), row(MLA_KV_LORA), row(MLA_ROPE), row(NSA_HEADS * 3)],
        out_shape=[jax.ShapeDtypeStruct((N, MLA_Q_LORA), F32), jax.ShapeDtypeStruct((N, MLA_KV_LORA), F32),
                   jax.ShapeDtypeStruct((N, MLA_ROPE), F32), jax.ShapeDtypeStruct((N, NSA_HEADS * 3), F32)],
        compiler_params=_params(("parallel",), 40),
        name="prep",
    )(h, cos32, sin32, q_norm, kv_norm)


def _mla_q_kernel(q_ref, c_ref, s_ref, w_ref, o_ref):
    nope = q_ref[:, 0:512]
    rope = q_ref[:, 512:768] * c_ref[...] + q_ref[:, 768:1024] * s_ref[...]
    a = jnp.concatenate([nope, rope], axis=-1).astype(BF16)
    o_ref[...] = _dot(a, w_ref[...])


def _mla_q(q1, cos256, sin256, w_abs, tm=512):
    N = q1.shape[0]
    tm = _tile(N, tm)
    row = lambda w: pl.BlockSpec((tm, w), lambda i: (i, 0))
    return pl.pallas_call(
        _mla_q_kernel,
        grid=(N // tm,),
        in_specs=[row(1024), row(256), row(256), pl.BlockSpec(w_abs.shape, lambda i: (0, 0))],
        out_specs=row(MLA_HEADS * MLA_SLOT),
        out_shape=jax.ShapeDtypeStruct((N, MLA_HEADS * MLA_SLOT), F32),
        compiler_params=_params(("parallel",), 40),
        name="mla_q",
    )(q1, cos256, sin256, w_abs)


CAUSAL_SEGMENTS = 4


def _causal_segments(qi, nq, tq, body):
    n = CAUSAL_SEGMENTS if nq % CAUSAL_SEGMENTS == 0 else 1
    per = nq // n
    for s in range(n):
        @pl.when((qi >= s * per) & (qi < (s + 1) * per))
        def _():
            body((s + 1) * per * tq)


def _top_blocks(score, jb, n_top):
    sel = jnp.zeros(score.shape, jnp.bool_)
    picks = []
    for _ in range(n_top):
        m = jnp.max(score, axis=-1, keepdims=True)
        idx = jnp.min(jnp.where(score == m, jb, 1 << 30), axis=-1, keepdims=True)
        hit = jb == idx
        sel = jnp.logical_or(sel, hit)
        score = jnp.where(hit, -3e38, score)
        picks.append(idx)
    return sel, picks


def _pnsa_kernel(q_ref, g_ref, ck_ref, cv_ref, sk_ref, sv_ref, wk_ref, wv_ref, o_ref, *, tq, T, n_cmp, n_blk):
    q0 = pl.program_id(1) * tq
    t = q0 + _iota((tq, 1), 0)
    slopes = _alibi(NSA_HEADS)
    ck = ck_ref[0].astype(BF16)
    cv = cv_ref[0].astype(BF16)
    jc = _iota((1, n_cmp), 1)
    d_c = t - ((jc + 1) * NSA_BLOCK - 1)
    m_c = d_c >= 0
    d_cf = d_c.astype(F32)
    imp = jnp.zeros((tq, n_cmp), F32)
    o_c = []
    for h in range(NSA_HEADS):
        qh = q_ref[0, h].astype(BF16)
        p = _masked_softmax(_dot_nt(qh, ck) * NSA_SCALE - slopes[h] * d_cf, m_c)
        imp = imp + p
        o_c.append(_dot(p.astype(BF16), cv))
    if n_blk > n_cmp:
        imp = jnp.concatenate([imp, jnp.zeros((tq, n_blk - n_cmp), F32)], axis=-1)
    jb = _iota((1, n_blk), 1)
    cur = t >> NSA_BLOCK_SHIFT
    forced = (jb == 0) | (jb == cur) | (jb == cur - 1)
    score = jnp.where(jb <= cur, jnp.where(forced, NSA_FORCED, imp), -NSA_FORCED)
    sel, _ = _top_blocks(score, jb, min(NSA_TOPK, n_blk))
    sel_bf = jnp.where(sel, 1.0, 0.0).astype(BF16)
    band = NSA_WINDOW + tq
    wk = wk_ref[0, pl.ds(pl.multiple_of(q0, tq), band), :].astype(BF16)
    wv = wv_ref[0, pl.ds(pl.multiple_of(q0, tq), band), :].astype(BF16)
    wpos = q0 - NSA_WINDOW + _iota((1, band), 1)
    d_w = t - wpos
    m_w = (d_w >= 0) & (d_w < NSA_WINDOW) & (wpos >= 0)
    d_wf = d_w.astype(F32)

    def finish(kt):
        nb = kt // NSA_BLOCK
        expand = ((_iota((nb, kt), 1) >> NSA_BLOCK_SHIFT) == _iota((nb, kt), 0))
        key_sel = _dot(sel_bf[:, 0:nb], jnp.where(expand, 1.0, 0.0).astype(BF16)) > 0.5
        d_s = t - _iota((1, kt), 1)
        m_s = key_sel & (d_s >= 0)
        d_sf = d_s.astype(F32)
        sk = sk_ref[0, 0:kt, :].astype(BF16)
        sv = sv_ref[0, 0:kt, :].astype(BF16)
        for h in range(NSA_HEADS):
            qh = q_ref[0, h].astype(BF16)
            p_s = _masked_softmax(_dot_nt(qh, sk) * NSA_SCALE - slopes[h] * d_sf, m_s)
            o_s = _dot(p_s.astype(BF16), sv)
            p_w = _masked_softmax(_dot_nt(qh, wk) * NSA_SCALE - slopes[h] * d_wf, m_w)
            o_w = _dot(p_w.astype(BF16), wv)
            g = g_ref[0, h]
            o_ref[0, h] = g[:, 0:1] * o_c[h] + g[:, 1:2] * o_s + g[:, 2:3] * o_w

    _causal_segments(pl.program_id(1), T // tq, tq, finish)


def _prompt_nsa(q4, g4, ck, cv, sk, sv, wkp, wvp, tq=128):
    B, H, T, dh = q4.shape
    n_cmp = ck.shape[1]
    n_blk = -(-T // NSA_BLOCK)
    assert T % tq == 0 and T % NSA_BLOCK == 0
    per_b = lambda shape: pl.BlockSpec((1,) + shape, lambda b, i: (b,) + (0,) * len(shape))
    return pl.pallas_call(
        functools.partial(_pnsa_kernel, tq=tq, T=T, n_cmp=n_cmp, n_blk=n_blk),
        grid=(B, T // tq),
        in_specs=[pl.BlockSpec((1, H, tq, dh), lambda b, i: (b, 0, i, 0)),
                  pl.BlockSpec((1, H, tq, 3), lambda b, i: (b, 0, i, 0)),
                  per_b((n_cmp, dh)), per_b((n_cmp, dh)), per_b((T, dh)), per_b((T, dh)),
                  per_b((T + NSA_WINDOW, dh)), per_b((T + NSA_WINDOW, dh))],
        out_specs=pl.BlockSpec((1, H, tq, dh), lambda b, i: (b, 0, i, 0)),
        out_shape=jax.ShapeDtypeStruct((B, H, T, dh), F32),
        compiler_params=_params(("parallel", "parallel"), 48),
        name="prompt_nsa",
    )(q4, g4, ck, cv, sk, sv, wkp, wvp)


def _diff_lambda(lp_ref):
    lp = lp_ref[...]
    a = jnp.sum(lp[0:1] * lp[1:2], axis=-1, keepdims=True)
    b = jnp.sum(lp[2:3] * lp[3:4], axis=-1, keepdims=True)
    return jnp.exp(a) - jnp.exp(b)


def _pdiff_kernel(q_ref, k1_ref, k2_ref, v_ref, lp_ref, g_ref, o_ref, *, tq, T, lam_init):
    q0 = pl.program_id(1) * tq
    lam = _diff_lambda(lp_ref) + lam_init
    slopes = _alibi(DIFF_HEADS)

    def body(kt):
        dist = (q0 + _iota((tq, 1), 0)) - _iota((1, kt), 1)
        mask = dist >= 0
        distf = dist.astype(F32)
        ks = (k1_ref[0, 0:kt, :].astype(BF16), k2_ref[0, 0:kt, :].astype(BF16))
        v = v_ref[0, 0:kt, :].astype(BF16)
        for h in range(DIFF_HEADS):
            ps = []
            for c in range(2):
                qh = q_ref[0, 2 * h + c].astype(BF16)
                ps.append(_masked_softmax(_dot_nt(qh, ks[c]) * DIFF_SCALE - slopes[h] * distf, mask))
            w = ps[0] - lam * ps[1]
            o = _dot(w.astype(BF16), v)
            o_ref[0, :, h * DIFF_DV:(h + 1) * DIFF_DV] = _rms(o, g_ref[...]) * (1.0 - lam_init)

    _causal_segments(pl.program_id(1), T // tq, tq, body)


def _prompt_diff(q5, k1, k2, v, lam_p, subln, lam_init, tq=128):
    B, HC, T, dh = q5.shape
    per_b = lambda shape: pl.BlockSpec((1,) + shape, lambda b, i: (b,) + (0,) * len(shape))
    return pl.pallas_call(
        functools.partial(_pdiff_kernel, tq=tq, T=T, lam_init=lam_init),
        grid=(B, T // tq),
        in_specs=[pl.BlockSpec((1, HC, tq, dh), lambda b, i: (b, 0, i, 0)),
                  per_b((T, dh)), per_b((T, dh)), per_b((T, DIFF_DV)),
                  pl.BlockSpec((4, DIFF_DH), lambda b, i: (0, 0)),
                  pl.BlockSpec((1, DIFF_DV), lambda b, i: (0, 0))],
        out_specs=pl.BlockSpec((1, tq, DIFF_HEADS * DIFF_DV), lambda b, i: (b, i, 0)),
        out_shape=jax.ShapeDtypeStruct((B, T, DIFF_HEADS * DIFF_DV), F32),
        compiler_params=_params(("parallel", "parallel"), 48),
        name="prompt_diff",
    )(q5, k1, k2, v, lam_p, subln)


def _pmla_kernel(q_ref, k_ref, o_ref, *, tq, T):
    q0 = pl.program_id(1) * tq

    def body(kt):
        mask = ((q0 + _iota((tq, 1), 0)) - _iota((1, kt), 1)) >= 0
        k = k_ref[0, 0:kt, :].astype(BF16)
        lat = k[:, 0:MLA_KV_LORA]
        for h in range(MLA_HEADS):
            qh = q_ref[0, :, h * MLA_SLOT:(h + 1) * MLA_SLOT].astype(BF16)
            p = _masked_softmax(_dot_nt(qh, k) * MLA_SCALE, mask)
            o_ref[0, :, h * MLA_KV_LORA:(h + 1) * MLA_KV_LORA] = _dot(p.astype(BF16), lat)

    _causal_segments(pl.program_id(1), T // tq, tq, body)


def _prompt_mla(q_cat, k_cat, tq=128):
    B, T, _ = q_cat.shape
    return pl.pallas_call(
        functools.partial(_pmla_kernel, tq=tq, T=T),
        grid=(B, T // tq),
        in_specs=[pl.BlockSpec((1, tq, MLA_HEADS * MLA_SLOT), lambda b, i: (b, i, 0)),
                  pl.BlockSpec((1, T, MLA_SLOT), lambda b, i: (b, 0, 0))],
        out_specs=pl.BlockSpec((1, tq, MLA_HEADS * MLA_KV_LORA), lambda b, i: (b, i, 0)),
        out_shape=jax.ShapeDtypeStruct((B, T, MLA_HEADS * MLA_KV_LORA), F32),
        compiler_params=_params(("parallel", "parallel"), 48),
        name="prompt_mla",
    )(q_cat, k_cat)


def _page_copies(pt_ref, cache_ref, buf_ref, sem_ref, layer, seq, chunk, slot, *, n_pages, ppc, transposed):
    copies = []
    for j in range(ppc):
        page = pt_ref[seq * n_pages + chunk * ppc + j]
        if transposed:
            dst = buf_ref.at[slot, :, pl.ds(j * PAGE, PAGE)]
        else:
            dst = buf_ref.at[slot, pl.ds(j * PAGE, PAGE), :]
        copies.append(pltpu.make_async_copy(cache_ref.at[layer, page], dst, sem_ref.at[slot]))
    return copies


def _paged_kernel(pt_ref, q_ref, new_ref, *rest, kind, layer, n_pages, ppc, past_len, n_tok, lam_init):
    if kind == "diff":
        lp_ref, g_ref, cache_ref, o_ref, buf_ref, sem_ref, m_ref, l_ref, acc_ref = rest
    else:
        cache_ref, o_ref, buf_ref, sem_ref, m_ref, l_ref, acc_ref = rest
    b = pl.program_id(0)
    c = pl.program_id(1)
    nb = pl.num_programs(0)
    nc = pl.num_programs(1)
    step = b * nc + c
    slot = step % 2
    copies = functools.partial(_page_copies, pt_ref, cache_ref, buf_ref, sem_ref, layer,
                               n_pages=n_pages, ppc=ppc, transposed=kind == "mla")

    @pl.when(step == 0)
    def _():
        for cp in copies(b, c, slot):
            cp.start()

    @pl.when(step + 1 < nb * nc)
    def _():
        nxt = step + 1
        for cp in copies(nxt // nc, nxt % nc, 1 - slot):
            cp.start()

    @pl.when(c == 0)
    def _():
        m_ref[...] = jnp.full(m_ref.shape, NEG_INF, F32)
        l_ref[...] = jnp.zeros(l_ref.shape, F32)
        acc_ref[...] = jnp.zeros(acc_ref.shape, F32)

    rows = q_ref.shape[1]
    q = q_ref[0].astype(BF16)
    r = _iota((rows, 1), 0)
    if kind == "diff":
        tok = (r % (n_tok * DIFF_HEADS)) // DIFF_HEADS
        head = r % DIFF_HEADS
        slope = jnp.zeros((rows, 1), F32)
        for h, sl in enumerate(_alibi(DIFF_HEADS)):
            slope = jnp.where(head == h, sl, slope)
        scale = DIFF_SCALE
    else:
        tok = r // MLA_HEADS
        slope = None
        scale = MLA_SCALE
    pos = past_len + tok

    def update(s, pv, kpos, valid):
        s = s * scale
        dist = pos - kpos
        if slope is not None:
            s = s - slope * dist.astype(F32)
        mask = dist >= 0
        if valid is not None:
            mask = mask & valid
        s = jnp.where(mask, s, NEG_INF)
        m_old = m_ref[...]
        m_new = jnp.maximum(m_old, jnp.max(s, axis=-1, keepdims=True))
        alpha = jnp.exp(m_old - m_new)
        p = jnp.where(mask, jnp.exp(s - m_new), 0.0)
        l_ref[...] = alpha * l_ref[...] + jnp.sum(p, axis=-1, keepdims=True)
        acc_ref[...] = alpha * acc_ref[...] + pv(p.astype(BF16))
        m_ref[...] = m_new

    for cp in copies(b, c, slot):
        cp.wait()
    keys = ppc * PAGE
    kpos = c * keys + _iota((1, keys), 1)
    if kind == "diff":
        k = buf_ref[slot, :, 0:2 * DIFF_DH].astype(BF16)
        v = buf_ref[slot, :, 2 * DIFF_DH:].astype(BF16)
        update(_dot_nt(q, k), lambda p: _dot(p, v), kpos, None)
    else:
        lat_t = buf_ref[slot, 0:MLA_KV_LORA, :].astype(BF16)
        kr_t = buf_ref[slot, MLA_KV_LORA:, :].astype(BF16)
        s = _dot(q[:, 0:MLA_KV_LORA], lat_t) + _dot(q[:, MLA_KV_LORA:MLA_KV_LORA + MLA_ROPE], kr_t)
        update(s, lambda p: _dot_nt(p, lat_t), kpos, None)

    @pl.when(c == nc - 1)
    def _():
        nk = new_ref.shape[1]
        jn = _iota((1, nk), 1)
        kw = new_ref.shape[2] if kind == "mla" else 2 * DIFF_DH
        vlo = 0 if kind == "mla" else 2 * DIFF_DH
        kn = new_ref[0, :, 0:kw].astype(BF16)
        vn = new_ref[0, :, vlo:vlo + 128].astype(BF16)
        update(_dot_nt(q[:, 0:kw], kn), lambda p: _dot(p, vn), past_len + jn, jn < n_tok)
        o = acc_ref[...] * (1.0 / jnp.maximum(l_ref[...], 1e-30))
        if kind == "diff":
            half = rows // 2
            lam = _diff_lambda(lp_ref) + lam_init
            w = o[0:half] - lam * o[half:rows]
            o_ref[0] = _rms(w, g_ref[...]) * (1.0 - lam_init)
        else:
            o_ref[0] = o


def _paged_attention(kind, q, new_rows, cache, page_table, layer, extra=(), lam_init=0.0, ppc=32):
    B, rows, qw = q.shape
    n_pages = page_table.shape[1]
    ppc = min(ppc, n_pages)
    if kind == "mla":
        assert n_pages % ppc == 0 and cache.shape[3] == PAGE
        buf_shape = (2, cache.shape[2], ppc * PAGE)
    else:
        assert n_pages % ppc == 0 and cache.shape[2] == PAGE
        buf_shape = (2, ppc * PAGE, cache.shape[3])
    n_tok = rows // (2 * DIFF_HEADS if kind == "diff" else MLA_HEADS)
    out_rows = rows // 2 if kind == "diff" else rows
    nc = n_pages // ppc
    in_specs = [pl.BlockSpec((1, rows, qw), lambda b, c, pt: (b, 0, 0)),
                pl.BlockSpec((1,) + new_rows.shape[1:], lambda b, c, pt: (b, 0, 0))]
    in_specs += [pl.BlockSpec(e.shape, lambda b, c, pt: (0, 0)) for e in extra]
    in_specs.append(pl.BlockSpec(memory_space=pl.ANY))
    grid_spec = pltpu.PrefetchScalarGridSpec(
        num_scalar_prefetch=1,
        grid=(B, nc),
        in_specs=in_specs,
        out_specs=pl.BlockSpec((1, out_rows, 128), lambda b, c, pt: (b, 0, 0)),
        scratch_shapes=[pltpu.VMEM(buf_shape, F32),
                        pltpu.SemaphoreType.DMA((2,)),
                        pltpu.VMEM((rows, 1), F32), pltpu.VMEM((rows, 1), F32), pltpu.VMEM((rows, 128), F32)],
    )
    return pl.pallas_call(
        functools.partial(_paged_kernel, kind=kind, layer=layer, n_pages=n_pages, ppc=ppc,
                          past_len=n_pages * PAGE, n_tok=n_tok, lam_init=lam_init),
        grid_spec=grid_spec,
        out_shape=jax.ShapeDtypeStruct((B, out_rows, 128), F32),
        compiler_params=_params(("arbitrary", "arbitrary"), 48),
        name="paged_" + kind,
    )(page_table.reshape(-1), q, new_rows, *extra, cache)


def _snsa_a_copies(pt_ref, cache_ref, buf_ref, sem_ref, layer, seq, slot, *, n_pages):
    copies = []
    for p in range(n_pages):
        page = pt_ref[seq * n_pages + p]
        copies.append(pltpu.make_async_copy(
            cache_ref.at[layer, page, 0:2],
            buf_ref.at[slot, :, :, p, :],
            sem_ref.at[slot]))
    return copies


def _snsa_a_kernel(pt_ref, q_ref, win_ref, w1_ref, b1_ref, w2k_ref, w2v_ref, cache_ref, oc_ref, ow_ref, top_ref,
                   buf_ref, sem_ref, *, layer, n_pages, past_len, n_tok):
    b = pl.program_id(0)
    nb = pl.num_programs(0)
    slot = b % 2
    copies = functools.partial(_snsa_a_copies, pt_ref, cache_ref, buf_ref, sem_ref, layer, n_pages=n_pages)

    @pl.when(b == 0)
    def _():
        for cp in copies(b, slot):
            cp.start()

    @pl.when(b + 1 < nb)
    def _():
        for cp in copies(b + 1, 1 - slot):
            cp.start()

    rows = q_ref.shape[1]
    q = q_ref[0].astype(BF16)
    r = _iota((rows, 1), 0)
    head = r % NSA_HEADS
    pos = past_len + r // NSA_HEADS
    slope = jnp.zeros((rows, 1), F32)
    for h, sl in enumerate(_alibi(NSA_HEADS)):
        slope = jnp.where(head == h, sl, slope)

    nw = win_ref.shape[1]
    wkv = win_ref[0].astype(BF16)
    wpos = past_len + n_tok - nw + _iota((1, nw), 1)
    d_w = pos - wpos
    m_w = (d_w >= 0) & (d_w < NSA_WINDOW)
    p_w = _masked_softmax(_dot_nt(q, wkv) * NSA_SCALE - slope * d_w.astype(F32), m_w)
    ow_ref[0] = _dot(p_w.astype(BF16), wkv)

    for cp in copies(b, slot):
        cp.wait()

    hid = []
    for c in range(2):
        acc = jnp.zeros((n_pages, 2 * NSA_CMP_HID), F32)
        for d in range(NSA_DH):
            acc = acc + _dot(buf_ref[slot, c, d].astype(BF16), w1_ref[c, d])
        hid.append(jax.nn.gelu(acc + b1_ref[c]).astype(BF16))
    halves = [_dot(hid[0][:, i * NSA_CMP_HID:(i + 1) * NSA_CMP_HID], w2k_ref[...])
              + _dot(hid[1][:, i * NSA_CMP_HID:(i + 1) * NSA_CMP_HID], w2v_ref[...]) for i in range(2)]
    ckv = jnp.concatenate(halves, axis=0).astype(BF16)
    n_cmp = 2 * n_pages
    col = _iota((1, n_cmp), 1)
    jc = 2 * (col % n_pages) + col // n_pages
    d_c = pos - ((jc + 1) * NSA_BLOCK - 1)
    p_c = _masked_softmax(_dot_nt(q, ckv) * NSA_SCALE - slope * d_c.astype(F32), d_c >= 0)
    oc_ref[0] = _dot(p_c.astype(BF16), ckv)

    imp = jnp.sum(p_c.reshape(n_tok, NSA_HEADS, n_cmp), axis=1)
    n_blk = -(-(past_len + n_tok) // NSA_BLOCK)
    pad = (-(n_cmp + 1)) % LANES + 1
    imp = jnp.concatenate([imp, jnp.zeros((n_tok, pad), F32)], axis=-1)
    jb = jnp.concatenate([jc, n_cmp + _iota((1, pad), 1)], axis=-1)
    cur = (past_len + _iota((n_tok, 1), 0)) >> NSA_BLOCK_SHIFT
    forced = (jb == 0) | (jb == cur) | (jb == cur - 1)
    score = jnp.where(jb <= cur, jnp.where(forced, NSA_FORCED, imp), -NSA_FORCED)
    score = jnp.where(jb < n_blk, score, -3e38)
    _, picks = _top_blocks(score, jb, NSA_TOPK)
    lane = _iota((n_tok, LANES), 1)
    top = jnp.zeros((n_tok, LANES), I32)
    for k, idx in enumerate(picks):
        top = jnp.where(lane == k, idx, top)
    top_ref[0] = top


def _sample_nsa_a(q, win_cat, w1t, b1t, w2k, w2v, cache_t, page_table, layer, n_tok):
    B, rows, _ = q.shape
    n_pages = page_table.shape[1]
    nw = win_cat.shape[1]
    const = lambda shape: pl.BlockSpec(shape, lambda b, pt: (0,) * len(shape))
    grid_spec = pltpu.PrefetchScalarGridSpec(
        num_scalar_prefetch=1,
        grid=(B,),
        in_specs=[pl.BlockSpec((1, rows, 128), lambda b, pt: (b, 0, 0)),
                  pl.BlockSpec((1, nw, 128), lambda b, pt: (b, 0, 0)),
                  const(w1t.shape), const(b1t.shape), const(w2k.shape), const(w2v.shape),
                  pl.BlockSpec(memory_space=pl.ANY)],
        out_specs=[pl.BlockSpec((1, rows, 128), lambda b, pt: (b, 0, 0)),
                   pl.BlockSpec((1, rows, 128), lambda b, pt: (b, 0, 0)),
                   pl.BlockSpec((1, n_tok, LANES), lambda b, pt: (b, 0, 0))],
        scratch_shapes=[pltpu.VMEM((2, 2, NSA_DH, n_pages, PAGE), F32), pltpu.SemaphoreType.DMA((2,))],
    )
    return pl.pallas_call(
        functools.partial(_snsa_a_kernel, layer=layer, n_pages=n_pages, past_len=n_pages * PAGE, n_tok=n_tok),
        grid_spec=grid_spec,
        out_shape=[jax.ShapeDtypeStruct((B, rows, 128), F32), jax.ShapeDtypeStruct((B, rows, 128), F32),
                   jax.ShapeDtypeStruct((B, n_tok, LANES), I32)],
        compiler_params=_params(("arbitrary",), 56),
        name="sample_nsa_a",
    )(page_table.reshape(-1), q, win_cat, w1t, b1t, w2k, w2v, cache_t)


def _snsa_b_copies(pt_ref, top_ref, cache_ref, buf_ref, sem_ref, layer, seq, slot, *, n_pages, n_tok):
    copies = []
    n_cmp = 2 * n_pages
    for t in range(n_tok):
        for k in range(NSA_TOPK):
            blk = top_ref[(seq * n_tok + t) * NSA_TOPK + k]
            blk = jnp.where(blk < n_cmp, blk, 0)
            page = pt_ref[seq * n_pages + (blk >> 1)]
            copies.append(pltpu.make_async_copy(
                cache_ref.at[layer, page, 2:4],
                buf_ref.at[slot, t, :, :, pl.ds(k * PAGE, PAGE)],
                sem_ref.at[slot]))
    return copies


def _snsa_b_kernel(pt_ref, top_ref, q_ref, nk_ref, nv_ref, oc_ref, ow_ref, g_ref, cache_ref, o_ref, buf_ref, sem_ref,
                   *, layer, n_pages, past_len, n_tok):
    b = pl.program_id(0)
    nb = pl.num_programs(0)
    slot = b % 2
    copies = functools.partial(_snsa_b_copies, pt_ref, top_ref, cache_ref, buf_ref, sem_ref, layer,
                               n_pages=n_pages, n_tok=n_tok)

    @pl.when(b == 0)
    def _():
        for cp in copies(b, slot):
            cp.start()

    @pl.when(b + 1 < nb)
    def _():
        for cp in copies(b + 1, 1 - slot):
            cp.start()

    for cp in copies(b, slot):
        cp.wait()

    n_cmp = 2 * n_pages
    keys = NSA_TOPK * PAGE
    lane = _iota((1, keys), 1)
    slot_of = lane // PAGE
    within = lane % PAGE
    jn = _iota((1, nk_ref.shape[2]), 1)
    new_k = nk_ref[0].astype(BF16)
    new_v = nv_ref[0].astype(BF16)
    slopes = jnp.zeros((NSA_HEADS, 1), F32)
    hrow = _iota((NSA_HEADS, 1), 0)
    for h, sl in enumerate(_alibi(NSA_HEADS)):
        slopes = jnp.where(hrow == h, sl, slopes)
    for t in range(n_tok):
        blk_of = jnp.zeros((1, keys), I32)
        has_new = jnp.int32(0)
        for k in range(NSA_TOPK):
            blk = top_ref[(b * n_tok + t) * NSA_TOPK + k]
            blk_of = jnp.where(slot_of == k, blk, blk_of)
            has_new = jnp.maximum(has_new, (blk == n_cmp).astype(I32))
        pos = past_len + t
        q = q_ref[0, t].astype(BF16)
        k_t = buf_ref[slot, t, 0].astype(BF16)
        v_t = buf_ref[slot, t, 1].astype(BF16)
        d_g = pos - ((blk_of >> 1) * PAGE + within)
        m_g = (d_g >= 0) & (blk_of < n_cmp) & ((within >> NSA_BLOCK_SHIFT) == (blk_of & 1))
        s_g = jnp.where(m_g, _dot(q, k_t) * NSA_SCALE - slopes * d_g.astype(F32), NEG_INF)
        d_n = pos - (past_len + jn)
        m_n = (d_n >= 0) & (jn < has_new * n_tok)
        s_n = jnp.where(m_n, _dot(q, new_k) * NSA_SCALE - slopes * d_n.astype(F32), NEG_INF)
        m = jnp.maximum(jnp.max(s_g, axis=-1, keepdims=True), jnp.max(s_n, axis=-1, keepdims=True))
        e_g = jnp.where(m_g, jnp.exp(s_g - m), 0.0)
        e_n = jnp.where(m_n, jnp.exp(s_n - m), 0.0)
        den = jnp.sum(e_g, axis=-1, keepdims=True) + jnp.sum(e_n, axis=-1, keepdims=True)
        o_s = (_dot_nt(e_g.astype(BF16), v_t) + _dot_nt(e_n.astype(BF16), new_v)) * (1.0 / jnp.maximum(den, 1e-30))
        g = g_ref[0, t]
        o_ref[0, t] = g[:, 0:1] * oc_ref[0, t] + g[:, 1:2] * o_s + g[:, 2:3] * ow_ref[0, t]


def _sample_nsa_b(q, new_k, new_v, o_c, o_w, gates, top, cache_t, page_table, layer):
    B, n_tok, H, dh = q.shape
    n_pages = page_table.shape[1]
    per_b = lambda shape: pl.BlockSpec((1,) + shape, lambda b, pt, tp: (b,) + (0,) * len(shape))
    grid_spec = pltpu.PrefetchScalarGridSpec(
        num_scalar_prefetch=2,
        grid=(B,),
        in_specs=[per_b((n_tok, H, dh)), per_b(new_k.shape[1:]), per_b(new_v.shape[1:]), per_b((n_tok, H, dh)),
                  per_b((n_tok, H, dh)), per_b((n_tok, H, 3)), pl.BlockSpec(memory_space=pl.ANY)],
        out_specs=per_b((n_tok, H, dh)),
        scratch_shapes=[pltpu.VMEM((2, n_tok, 2, dh, NSA_TOPK * PAGE), F32), pltpu.SemaphoreType.DMA((2,))],
    )
    return pl.pallas_call(
        functools.partial(_snsa_b_kernel, layer=layer, n_pages=n_pages, past_len=n_pages * PAGE, n_tok=n_tok),
        grid_spec=grid_spec,
        out_shape=jax.ShapeDtypeStruct((B, n_tok, H, dh), F32),
        compiler_params=_params(("arbitrary",), 56),
        name="sample_nsa_b",
    )(page_table.reshape(-1), top.reshape(-1), q, new_k, new_v, o_c, o_w, gates, cache_t)


def _out_ln_kernel(a_ref, w_ref, x_ref, g_ref, b_ref, y_ref, yb_ref, *, alpha):
    mix = _dot(a_ref[...].astype(BF16), w_ref[...])
    y = _layer_norm(alpha * x_ref[...] + mix, g_ref[...], b_ref[...])
    y_ref[...] = y
    yb_ref[...] = y.astype(BF16)


def _out_ln(a, w_bf16, x, g, b, alpha, tm=256):
    N, K = a.shape
    tm = _tile(N, tm)
    D = w_bf16.shape[1]
    row = lambda w: pl.BlockSpec((tm, w), lambda i: (i, 0))
    const = lambda shape: pl.BlockSpec(shape, lambda i: (0, 0))
    return pl.pallas_call(
        functools.partial(_out_ln_kernel, alpha=alpha),
        grid=(N // tm,),
        in_specs=[row(K), const((K, D)), row(D), const((1, D)), const((1, D))],
        out_specs=[row(D), row(D)],
        out_shape=[jax.ShapeDtypeStruct((N, D), F32), jax.ShapeDtypeStruct((N, D), BF16)],
        compiler_params=_params(("parallel",), 48),
        name="out_ln",
    )(a, w_bf16, x, g, b)


def _split_bf16(v):
    hi = v.astype(BF16)
    lo = (v - hi.astype(F32)).astype(BF16)
    return hi, lo


def _route_kernel(x_ref, w_ref, b_ref, gate_ref):
    xh, xl = _split_bf16(x_ref[...])
    wh, wl = _split_bf16(w_ref[...])
    logits = _dot(xh, wh) + (_dot(xh, wl) + _dot(xl, wh)) + b_ref[...]
    tm = logits.shape[0]
    lane = _iota((tm, LANES), 1)
    is_grp = (lane >= N_EXPERTS) & (lane < N_EXPERTS + N_GROUPS)
    lg = jnp.where(is_grp, logits, NEG_INF)
    mg = jnp.max(lg, axis=-1, keepdims=True)
    eg = jnp.where(is_grp, jnp.exp(lg - mg), 0.0)
    pg = eg / jnp.sum(eg, axis=-1, keepdims=True)
    g_w = jnp.max(pg, axis=-1, keepdims=True)
    g_idx = jnp.min(jnp.where(is_grp & (pg == g_w), lane, 1 << 30), axis=-1, keepdims=True) - N_EXPERTS
    in_grp = (lane >= g_idx * EXP_PER_GROUP) & (lane < (g_idx + 1) * EXP_PER_GROUP)
    le = jnp.where(in_grp, logits, NEG_INF)
    me = jnp.max(le, axis=-1, keepdims=True)
    ee = jnp.where(in_grp, jnp.exp(le - me), 0.0)
    pe = ee / jnp.sum(ee, axis=-1, keepdims=True)
    pe = jnp.where(in_grp, pe, -1.0)
    e1 = jnp.max(pe, axis=-1, keepdims=True)
    i1 = jnp.min(jnp.where(pe == e1, lane, 1 << 30), axis=-1, keepdims=True)
    pe2 = jnp.where(lane == i1, -1.0, pe)
    e2 = jnp.max(pe2, axis=-1, keepdims=True)
    i2 = jnp.min(jnp.where(pe2 == e2, lane, 1 << 30), axis=-1, keepdims=True)
    tot = e1 + e2
    gate_ref[...] = jnp.where(lane == i1, g_w * e1 / tot, jnp.where(lane == i2, g_w * e2 / tot, 0.0))


def _route(x, w_route, b_route, tm=512):
    N, D = x.shape
    tm = _tile(N, tm)
    return pl.pallas_call(
        _route_kernel,
        grid=(N // tm,),
        in_specs=[pl.BlockSpec((tm, D), lambda i: (i, 0)), pl.BlockSpec((D, LANES), lambda i: (0, 0)),
                  pl.BlockSpec((1, LANES), lambda i: (0, 0))],
        out_specs=pl.BlockSpec((tm, LANES), lambda i: (i, 0)),
        out_shape=jax.ShapeDtypeStruct((N, LANES), F32),
        compiler_params=_params(("parallel",), 40),
        name="route",
    )(x, w_route, b_route)


MOE_ROW_CHUNKS = 4


def _moe_kernel(x_ref, gate_ref, w1_ref, w3_ref, w2_ref, o_ref):
    e = pl.program_id(1)
    w1 = w1_ref[0].astype(BF16)
    w3 = w3_ref[0].astype(BF16)
    w2 = w2_ref[0].astype(BF16)
    rc = x_ref.shape[0] // MOE_ROW_CHUNKS
    for r in range(MOE_ROW_CHUNKS):
        rows = pl.ds(r * rc, rc)
        x = x_ref[rows, :]
        gate = gate_ref[rows, :]
        g = jnp.sum(jnp.where(_iota(gate.shape, 1) == e, gate, 0.0), axis=-1, keepdims=True)
        h = (jax.nn.silu(_dot(x, w1)) * _dot(x, w3)) * g
        y = _dot(h.astype(BF16), w2)

        @pl.when(e == 0)
        def _():
            o_ref[rows, :] = y

        @pl.when(e > 0)
        def _():
            o_ref[rows, :] += y


def _moe(xb, gate, w1, w3, w2, tm=1024):
    N, D = xb.shape
    tm = _tile(N, tm)
    assert tm % (8 * MOE_ROW_CHUNKS) == 0
    E, _, Hd = w1.shape
    return pl.pallas_call(
        _moe_kernel,
        grid=(N // tm, E),
        in_specs=[pl.BlockSpec((tm, D), lambda i, e: (i, 0)), pl.BlockSpec((tm, LANES), lambda i, e: (i, 0)),
                  pl.BlockSpec((1, D, Hd), lambda i, e: (e, 0, 0)), pl.BlockSpec((1, D, Hd), lambda i, e: (e, 0, 0)),
                  pl.BlockSpec((1, Hd, D), lambda i, e: (e, 0, 0))],
        out_specs=pl.BlockSpec((tm, D), lambda i, e: (i, 0)),
        out_shape=jax.ShapeDtypeStruct((N, D), F32),
        compiler_params=_params(("parallel", "arbitrary"), 56),
        name="moe",
    )(xb, gate, w1, w3, w2)


def _res_ln_kernel(x_ref, y_ref, g_ref, b_ref, o_ref, *, alpha):
    o_ref[...] = _layer_norm(alpha * x_ref[...] + y_ref[...], g_ref[...], b_ref[...])


def _res_ln(x, y, g, b, alpha, tm=512):
    N, D = x.shape
    tm = _tile(N, tm)
    row = pl.BlockSpec((tm, D), lambda i: (i, 0))
    const = pl.BlockSpec((1, D), lambda i: (0, 0))
    return pl.pallas_call(
        functools.partial(_res_ln_kernel, alpha=alpha),
        grid=(N // tm,),
        in_specs=[row, row, const, const],
        out_specs=row,
        out_shape=jax.ShapeDtypeStruct((N, D), F32),
        compiler_params=_params(("parallel",), 40),
        name="res_ln",
    )(x, y, g, b)


def _layer_weights(w_in, pe, w1, w2, w_uq, w_uk, w_uv, wg, bg, we, be):
    D = w_in.shape[0]
    o = np.cumsum([0, 512, 64, 64, 64, 64, 64, 64, 24, 512, 128, 128, 384, 128, 32])
    seg = lambda i: w_in[:, o[i]:o[i + 1]]
    kr = seg(13)
    half = MLA_ROPE // 2
    w_in_p = jnp.concatenate(
        [seg(0), seg(1), seg(2), seg(3), seg(4), seg(5), seg(6), seg(8), seg(9), seg(10), seg(11), seg(12), kr,
         jnp.concatenate([kr[:, half:], kr[:, :half]], axis=1), seg(7),
         jnp.zeros((D, H_COLS - C_GATE - 24), F32)], axis=1)
    w1t = []
    for c in range(2):
        a = w1[c].reshape(NSA_BLOCK, NSA_DH, NSA_CMP_HID).transpose(1, 0, 2)
        z = jnp.zeros_like(a)
        w1t.append(jnp.concatenate([jnp.concatenate([a, z], axis=2), jnp.concatenate([z, a], axis=2)], axis=1))
    w1t = jnp.stack(w1t)
    z2 = jnp.zeros_like(w2[0])
    w2k = jnp.concatenate([w2[0], z2], axis=1)
    w2v = jnp.concatenate([z2, w2[1]], axis=1)
    uq = w_uq.reshape(MLA_Q_LORA, MLA_HEADS, MLA_NOPE + MLA_ROPE)
    uq_r = uq[:, :, MLA_NOPE:]
    uq_rs = jnp.concatenate([uq_r[:, :, half:], uq_r[:, :, :half]], axis=2)
    w_uq_p = jnp.concatenate([uq[:, :, :MLA_NOPE].reshape(MLA_Q_LORA, -1), uq_r.reshape(MLA_Q_LORA, -1),
                              uq_rs.reshape(MLA_Q_LORA, -1)], axis=1)
    eye_h = jnp.eye(MLA_HEADS, dtype=F32)
    top = w_uk.transpose(1, 2, 0)[:, :, None, :] * eye_h[:, None, :, None]
    top = jnp.pad(top, ((0, 0), (0, 0), (0, 0), (0, MLA_SLOT - MLA_KV_LORA)))
    bot = eye_h[:, None, :, None] * jnp.eye(MLA_ROPE, dtype=F32)[None, :, None, :]
    bot = jnp.pad(bot, ((0, 0), (0, 0), (0, 0), (MLA_KV_LORA, MLA_SLOT - MLA_KV_LORA - MLA_ROPE)))
    w_abs = jnp.concatenate([top.reshape(MLA_HEADS * MLA_NOPE, -1), bot.reshape(MLA_HEADS * MLA_ROPE, -1)], axis=0)
    w_uv_bd = (w_uv.transpose(1, 0, 2)[:, :, None, :] * eye_h[:, None, :, None]).reshape(
        MLA_HEADS * MLA_KV_LORA, MLA_HEADS * MLA_DV)
    w_route = jnp.concatenate([we, wg, jnp.zeros((D, LANES - N_EXPERTS - N_GROUPS), F32)], axis=1)
    b_route = jnp.concatenate([be, bg, jnp.zeros((LANES - N_EXPERTS - N_GROUPS,), F32)])[None, :]
    return dict(w_in_p=w_in_p, w1t=w1t.astype(BF16), w2k=w2k.astype(BF16), w2v=w2v.astype(BF16), w_uq_p=w_uq_p,
                w_abs=w_abs.astype(BF16), w_uv_bd=w_uv_bd, w_route=w_route, b_route=b_route)


def _rope_tables(pos):
    half = MLA_ROPE // 2
    inv = ROPE_BASE ** (-jnp.arange(half, dtype=F32) / half)
    ang = pos.astype(F32)[:, None] * inv[None, :]
    cos, sin = jnp.cos(ang), jnp.sin(ang)
    cos32 = jnp.concatenate([cos, cos], axis=1)
    sin32 = jnp.concatenate([-sin, sin], axis=1)
    return cos32, sin32, jnp.tile(cos32, (1, MLA_HEADS)), jnp.tile(sin32, (1, MLA_HEADS))


def _layer(x_all, tables, lw, wl, layer, depth, Bp, T, Bs, Ts, cache_nsa_t, cache_diff, cache_mla_t, win_state,
           page_table):
    cos32, sin32, cos256, sin256 = tables
    Np = Bp * T
    alpha = (2 * depth) ** 0.25
    lam_init = 0.8 - 0.6 * math.exp(-0.3 * layer)
    n_pages = page_table.shape[1]
    past_len = n_pages * PAGE

    h = _mm(x_all, wl["w_in_p"], 1024, 256)
    cqn, lat, krope, gates = _prep(h, cos32, sin32, lw["mla_q_norm"][None, :], lw["mla_kv_norm"][None, :])
    q1 = _mm(cqn, wl["w_uq_p"], 1024, 1024)
    q_cat = _mla_q(q1, cos256, sin256, wl["w_abs"])

    nsa_rows = h[:, C_NSA_ROWS:C_WIN]
    win_rows = h[:, C_WIN:C_DIFF_Q]
    diff_rows = h[:, C_DIFF_KV:C_MLA_CQ]
    mla_rows = jnp.concatenate([lat, krope], axis=1)

    hp = h[:Np].reshape(Bp, T, H_COLS)
    rows_p = nsa_rows[:Np].reshape(Bp, T, 4, NSA_DH)
    n_cmp = T // NSA_BLOCK
    pe = lw["nsa_cmp_pe"].reshape(2, 1, NSA_BLOCK * NSA_DH)
    cmp_p = []
    for i in range(2):
        xin = rows_p[:, :n_cmp * NSA_BLOCK, i].reshape(Bp * n_cmp, NSA_BLOCK * NSA_DH)
        hid = _mm(xin, lw["nsa_cmp_w1"][i], 128, 128, a_bias=pe[i], act="gelu")
        cmp_p.append(_mm(hid, lw["nsa_cmp_w2"][i], 128, NSA_DH).reshape(Bp, n_cmp, NSA_DH))
    q4 = hp[:, :, C_NSA_Q:C_NSA_ROWS].reshape(Bp, T, NSA_HEADS, NSA_DH).transpose(0, 2, 1, 3)
    g4 = gates[:Np].reshape(Bp, T, NSA_HEADS, 3).transpose(0, 2, 1, 3)
    win_p = win_rows[:Np].reshape(Bp, T, 2, NSA_DH)
    win_pad = jnp.pad(win_p, ((0, 0), (NSA_WINDOW, 0), (0, 0), (0, 0)))
    o_nsa_p = _prompt_nsa(q4, g4, cmp_p[0], cmp_p[1], rows_p[:, :, 2], rows_p[:, :, 3],
                          win_pad[:, :, 0], win_pad[:, :, 1])
    o_nsa_p = o_nsa_p.transpose(0, 2, 1, 3).reshape(Np, NSA_HEADS * NSA_DH)
    q5 = hp[:, :, C_DIFF_Q:C_DIFF_KV].reshape(Bp, T, 2 * DIFF_HEADS, DIFF_DH).transpose(0, 2, 1, 3)
    dkv_p = diff_rows[:Np].reshape(Bp, T, 2 * DIFF_DH + DIFF_DV)
    o_diff_p = _prompt_diff(q5, dkv_p[:, :, 0:DIFF_DH], dkv_p[:, :, DIFF_DH:2 * DIFF_DH], dkv_p[:, :, 2 * DIFF_DH:],
                            lw["diff_lambda"], lw["diff_subln"][None, :], lam_init).reshape(Np, -1)
    k_cat_p = jnp.pad(mla_rows[:Np], ((0, 0), (0, MLA_SLOT - mla_rows.shape[1]))).reshape(Bp, T, MLA_SLOT)
    o_lat_p = _prompt_mla(q_cat[:Np].reshape(Bp, T, -1), k_cat_p).reshape(Np, -1)

    hs = h[Np:].reshape(Bs, Ts, H_COLS)
    pad_new = lambda a: jnp.pad(a, ((0, 0), (0, PAGE - Ts), (0, 0)))
    qd = hs[:, :, C_DIFF_Q:C_DIFF_KV].reshape(Bs, Ts, DIFF_HEADS, 2, DIFF_DH).transpose(0, 3, 1, 2, 4)
    qd = qd.reshape(Bs, 2, Ts * DIFF_HEADS, DIFF_DH)
    zq = jnp.zeros_like(qd[:, 0])
    q_diff = jnp.concatenate([jnp.concatenate([qd[:, 0], zq], axis=-1), jnp.concatenate([zq, qd[:, 1]], axis=-1)], axis=1)
    o_diff_s = _paged_attention("diff", q_diff, pad_new(diff_rows[Np:].reshape(Bs, Ts, -1)), cache_diff, page_table,
                                layer, extra=(lw["diff_lambda"], lw["diff_subln"][None, :]), lam_init=lam_init)
    o_diff_s = o_diff_s.reshape(Bs * Ts, DIFF_HEADS * DIFF_DV)
    q_mla = q_cat[Np:].reshape(Bs, Ts * MLA_HEADS, MLA_SLOT)
    o_lat_s = _paged_attention("mla", q_mla, pad_new(mla_rows[Np:].reshape(Bs, Ts, -1)), cache_mla_t, page_table,
                               layer)
    o_lat_s = o_lat_s.reshape(Bs * Ts, MLA_HEADS * MLA_KV_LORA)
    qn = hs[:, :, C_NSA_Q:C_NSA_ROWS].reshape(Bs, Ts, NSA_HEADS, NSA_DH)
    qn_pad = jnp.concatenate([qn, jnp.zeros_like(qn)], axis=-1)
    win_new = win_rows[Np:].reshape(Bs, Ts, 2 * NSA_DH)
    win_cat = jnp.concatenate([win_state.reshape(Bs, -1, 2 * NSA_DH), win_new], axis=1)
    pe_bias = [_mm(jnp.broadcast_to(pe[i], (8, NSA_BLOCK * NSA_DH)), lw["nsa_cmp_w1"][i], 8, 128)[0:1] for i in range(2)]
    b1t = jnp.stack([jnp.concatenate([pb, pb], axis=1) for pb in pe_bias])
    o_c, o_w, top = _sample_nsa_a(qn_pad.reshape(Bs, Ts * NSA_HEADS, 128), win_cat, wl["w1t"], b1t, wl["w2k"],
                                  wl["w2v"], cache_nsa_t, page_table, layer, Ts)
    top = top[:, :, :NSA_TOPK]
    new_sel = pad_new(nsa_rows[Np:].reshape(Bs, Ts, 4 * NSA_DH)[:, :, 2 * NSA_DH:]).transpose(0, 2, 1)
    g_s = gates[Np:].reshape(Bs, Ts, NSA_HEADS, 3)
    o_nsa_s = _sample_nsa_b(qn, new_sel[:, :NSA_DH], new_sel[:, NSA_DH:],
                            o_c[..., NSA_DH:].reshape(Bs, Ts, NSA_HEADS, NSA_DH),
                            o_w[..., NSA_DH:].reshape(Bs, Ts, NSA_HEADS, NSA_DH),
                            g_s, top, cache_nsa_t, page_table, layer)
    o_nsa_s = o_nsa_s.reshape(Bs * Ts, NSA_HEADS * NSA_DH)

    o_lat = jnp.concatenate([o_lat_p, o_lat_s], axis=0)
    o_mla = _mm(o_lat, wl["w_uv_bd"], 1024, 1024)
    mix = jnp.concatenate([jnp.concatenate([o_nsa_p, o_nsa_s], axis=0),
                           jnp.concatenate([o_diff_p, o_diff_s], axis=0), o_mla], axis=1)
    x1, x1b = _out_ln(mix, lw["w_out"].astype(BF16), x_all, lw["ln1_g"][None, :], lw["ln1_b"][None, :], alpha)
    gate = _route(x1, wl["w_route"], wl["b_route"])
    y = _moe(x1b, gate, lw["moe_w1"], lw["moe_w3"], lw["moe_w2"])
    x2 = _res_ln(x1, y, lw["ln2_g"][None, :], lw["ln2_b"][None, :], alpha)

    new_win_p = win_p[:, T - min(NSA_WINDOW, T):]
    new_win_s = win_cat[:, win_cat.shape[1] - min(NSA_WINDOW, win_cat.shape[1]):].reshape(Bs, -1, 2, NSA_DH)
    rows = dict(
        nsa_p=rows_p, nsa_s=nsa_rows[Np:].reshape(Bs, Ts, 4, NSA_DH),
        diff_p=dkv_p, diff_s=diff_rows[Np:].reshape(Bs, Ts, -1),
        mla_p=mla_rows[:Np].reshape(Bp, T, -1), mla_s=mla_rows[Np:].reshape(Bs, Ts, -1),
        win_p=new_win_p, win_s=new_win_s)
    return x2, rows


def kernel(x_prompt, x_sample, cache_nsa, cache_diff, cache_mla, state_nsa_win, page_table, w_in, nsa_cmp_pe, nsa_cmp_w1, nsa_cmp_w2, diff_lambda, diff_subln, mla_q_norm, mla_w_uq, mla_kv_norm, mla_w_uk, mla_w_uv, w_out, ln1_g, ln1_b, moe_w_group, moe_b_group, moe_w_expert, moe_b_expert, moe_w1, moe_w3, moe_w2, ln2_g, ln2_b):
    Bp, T, D = x_prompt.shape
    Bs, Ts, _ = x_sample.shape
    depth = w_in.shape[0]
    past_len = page_table.shape[1] * PAGE
    pos = jnp.concatenate([jnp.tile(jnp.arange(T, dtype=I32), Bp),
                           jnp.tile(past_len + jnp.arange(Ts, dtype=I32), Bs)])
    tables = _rope_tables(pos)
    x_all = jnp.concatenate([x_prompt.reshape(Bp * T, D), x_sample.reshape(Bs * Ts, D)], axis=0)
    cache_nsa_t = cache_nsa.transpose(0, 1, 3, 4, 2)
    cache_mla_t = cache_mla.transpose(0, 1, 3, 2)
    outs = []
    for l in range(depth):
        lw = dict(nsa_cmp_pe=nsa_cmp_pe[l], nsa_cmp_w1=nsa_cmp_w1[l], nsa_cmp_w2=nsa_cmp_w2[l],
                  diff_lambda=diff_lambda[l], diff_subln=diff_subln[l], mla_q_norm=mla_q_norm[l],
                  mla_kv_norm=mla_kv_norm[l], w_out=w_out[l], ln1_g=ln1_g[l], ln1_b=ln1_b[l],
                  moe_w1=moe_w1[l], moe_w3=moe_w3[l], moe_w2=moe_w2[l], ln2_g=ln2_g[l], ln2_b=ln2_b[l])
        wl = _layer_weights(w_in[l], nsa_cmp_pe[l], nsa_cmp_w1[l], nsa_cmp_w2[l], mla_w_uq[l], mla_w_uk[l],
                            mla_w_uv[l], moe_w_group[l], moe_b_group[l], moe_w_expert[l], moe_b_expert[l])
        x_all, rows = _layer(x_all, tables, lw, wl, l, depth, Bp, T, Bs, Ts, cache_nsa_t, cache_diff, cache_mla_t,
                             state_nsa_win[l], page_table)
        outs.append(rows)
    Np = Bp * T
    st = lambda k: jnp.stack([r[k] for r in outs])
    return (x_all[:Np].reshape(Bp, T, D), x_all[Np:].reshape(Bs, Ts, D),
            st("nsa_p"), st("nsa_s"), st("diff_p"), st("diff_s"), st("mla_p"), st("mla_s"), st("win_p"), st("win_s"))
```

```python
import functools
import math

import numpy as np
import jax
import jax.numpy as jnp
from jax import lax
from jax.experimental import pallas as pl
from jax.experimental.pallas import tpu as pltpu

F32 = jnp.float32
BF16 = jnp.bfloat16
I32 = jnp.int32

NSA_DH = 64
NSA_HEADS = 8
NSA_BLOCK = 64
NSA_BLOCK_SHIFT = 6
NSA_TOPK = 16
NSA_WINDOW = 512
NSA_CMP_HID = 128
NSA_FORCED = 1e4
NSA_SCALE = NSA_DH ** -0.5
DIFF_DH = 64
DIFF_DV = 128
DIFF_HEADS = 4
DIFF_SCALE = DIFF_DH ** -0.5
MLA_DV = 128
MLA_HEADS = 8
MLA_Q_LORA = 384
MLA_KV_LORA = 128
MLA_NOPE = 64
MLA_ROPE = 32
MLA_SLOT = 256
MLA_SCALE = (MLA_NOPE + MLA_ROPE) ** -0.5
ROPE_BASE = 10000.0
N_GROUPS = 4
EXP_PER_GROUP = 8
N_EXPERTS = 32
LN_EPS = 1e-5
RMS_EPS = 1e-6
NEG_INF = -1e30
LANES = 128
PAGE = 128
MIB = 2 ** 20

C_NSA_Q = 0
C_NSA_ROWS = 512
C_WIN = 768
C_DIFF_Q = 896
C_DIFF_KV = 1408
C_MLA_CQ = 1664
C_MLA_CKV = 2048
C_MLA_KR = 2176
C_MLA_KRS = 2208
C_GATE = 2240
H_COLS = 2304


def _alibi(n):
    return [2.0 ** (-8.0 * (i + 1) / n) for i in range(n)]


def _dot(a, b):
    return jnp.dot(a, b, preferred_element_type=F32)


def _dot_nt(a, b):
    return lax.dot_general(a, b, (((1,), (1,)), ((), ())), preferred_element_type=F32)


def _iota(shape, dim):
    return lax.broadcasted_iota(I32, shape, dim)


def _masked_softmax(s, mask):
    s = jnp.where(mask, s, NEG_INF)
    m = jnp.max(s, axis=-1, keepdims=True)
    e = jnp.where(mask, jnp.exp(s - m), 0.0)
    d = jnp.maximum(jnp.sum(e, axis=-1, keepdims=True), 1e-30)
    return e * (1.0 / d)


def _layer_norm(v, g, b):
    mu = jnp.mean(v, axis=-1, keepdims=True)
    c = v - mu
    var = jnp.mean(c * c, axis=-1, keepdims=True)
    return c * lax.rsqrt(var + LN_EPS) * g + b


def _rms(v, g):
    return v * lax.rsqrt(jnp.mean(v * v, axis=-1, keepdims=True) + RMS_EPS) * g


def _tile(n, pref):
    for t in range(min(n, pref), 7, -1):
        if n % t == 0 and t % 8 == 0:
            return t
    return n


def _params(sem, vmem_mib):
    return pltpu.CompilerParams(dimension_semantics=sem, vmem_limit_bytes=vmem_mib * MIB)


def _mm_kernel(a_ref, b_ref, *rest, has_abias, has_obias, act):
    rest = list(rest)
    a = a_ref[...]
    if has_abias:
        a = a + rest.pop(0)[...]
    acc = _dot(a.astype(BF16), b_ref[...].astype(BF16))
    if has_obias:
        acc = acc + rest.pop(0)[...]
    if act == "gelu":
        acc = jax.nn.gelu(acc)
    rest[0][...] = acc


def _mm(a, b, tm, tn, a_bias=None, o_bias=None, act=None, vmem_mib=48):
    M, K = a.shape
    N = b.shape[1]
    tm, tn = _tile(M, tm), min(tn, N)
    assert M % tm == 0 and N % tn == 0
    in_specs = [pl.BlockSpec((tm, K), lambda i, j: (i, 0)), pl.BlockSpec((K, tn), lambda i, j: (0, j))]
    args = [a, b]
    if a_bias is not None:
        in_specs.append(pl.BlockSpec((1, K), lambda i, j: (0, 0)))
        args.append(a_bias)
    if o_bias is not None:
        in_specs.append(pl.BlockSpec((1, tn), lambda i, j: (0, j)))
        args.append(o_bias)
    return pl.pallas_call(
        functools.partial(_mm_kernel, has_abias=a_bias is not None, has_obias=o_bias is not None, act=act),
        grid=(M // tm, N // tn),
        in_specs=in_specs,
        out_specs=pl.BlockSpec((tm, tn), lambda i, j: (i, j)),
        out_shape=jax.ShapeDtypeStruct((M, N), F32),
        compiler_params=_params(("parallel", "parallel"), vmem_mib),
        name="mm",
    )(*args)


def _prep_kernel(h_ref, c_ref, s_ref, qn_ref, kvn_ref, cqn_ref, lat_ref, kr_ref, gate_ref):
    cq = h_ref[:, C_MLA_CQ:C_MLA_CKV]
    cqn_ref[...] = _rms(cq, qn_ref[...])
    ckv = h_ref[:, C_MLA_CKV:C_MLA_KR]
    lat_ref[...] = _rms(ckv, kvn_ref[...])
    kr = h_ref[:, C_MLA_KR:C_MLA_KRS]
    krs = h_ref[:, C_MLA_KRS:C_GATE]
    kr_ref[...] = kr * c_ref[...] + krs * s_ref[...]
    gate_ref[...] = jax.nn.sigmoid(h_ref[:, C_GATE:C_GATE + NSA_HEADS * 3])


def _prep(h, cos32, sin32, q_norm, kv_norm, tm=512):
    N = h.shape[0]
    tm = _tile(N, tm)
    row = lambda w: pl.BlockSpec((tm, w), lambda i: (i, 0))
    const = lambda w: pl.BlockSpec((1, w), lambda i: (0, 0))
    return pl.pallas_call(
        _prep_kernel,
        grid=(N // tm,),
        in_specs=[row(H_COLS), row(MLA_ROPE), row(MLA_ROPE), const(MLA_Q_LORA), const(MLA_KV_LORA)],
        out_specs=[row(MLA_Q_LORA), row(MLA_KV_LORA), row(MLA_ROPE), row(NSA_HEADS * 3)],
        out_shape=[jax.ShapeDtypeStruct((N, MLA_Q_LORA), F32), jax.ShapeDtypeStruct((N, MLA_KV_LORA), F32),
                   jax.ShapeDtypeStruct((N, MLA_ROPE), F32), jax.ShapeDtypeStruct((N, NSA_HEADS * 3), F32)],
        compiler_params=_params(("parallel",), 40),
        name="prep",
    )(h, cos32, sin32, q_norm, kv_norm)


def _mla_q_kernel(q_ref, c_ref, s_ref, w_ref, o_ref):
    nope = q_ref[:, 0:512]
    rope = q_ref[:, 512:768] * c_ref[...] + q_ref[:, 768:1024] * s_ref[...]
    a = jnp.concatenate([nope, rope], axis=-1).astype(BF16)
    o_ref[...] = _dot(a, w_ref[...])


def _mla_q(q1, cos256, sin256, w_abs, tm=512):
    N = q1.shape[0]
    tm = _tile(N, tm)
    row = lambda w: pl.BlockSpec((tm, w), lambda i: (i, 0))
    return pl.pallas_call(
        _mla_q_kernel,
        grid=(N // tm,),
        in_specs=[row(1024), row(256), row(256), pl.BlockSpec(w_abs.shape, lambda i: (0, 0))],
        out_specs=row(MLA_HEADS * MLA_SLOT),
        out_shape=jax.ShapeDtypeStruct((N, MLA_HEADS * MLA_SLOT), F32),
        compiler_params=_params(("parallel",), 40),
        name="mla_q",
    )(q1, cos256, sin256, w_abs)


CAUSAL_SEGMENTS = 4


def _causal_segments(qi, nq, tq, body):
    n = CAUSAL_SEGMENTS if nq % CAUSAL_SEGMENTS == 0 else 1
    per = nq // n
    for s in range(n):
        @pl.when((qi >= s * per) & (qi < (s + 1) * per))
        def _():
            body((s + 1) * per * tq)


def _top_blocks(score, jb, n_top):
    sel = jnp.zeros(score.shape, jnp.bool_)
    picks = []
    for _ in range(n_top):
        m = jnp.max(score, axis=-1, keepdims=True)
        idx = jnp.min(jnp.where(score == m, jb, 1 << 30), axis=-1, keepdims=True)
        hit = jb == idx
        sel = jnp.logical_or(sel, hit)
        score = jnp.where(hit, -3e38, score)
        picks.append(idx)
    return sel, picks


def _pnsa_kernel(q_ref, g_ref, ck_ref, cv_ref, sk_ref, sv_ref, wk_ref, wv_ref, o_ref, *, tq, T, n_cmp, n_blk):
    q0 = pl.program_id(1) * tq
    t = q0 + _iota((tq, 1), 0)
    slopes = _alibi(NSA_HEADS)
    ck = ck_ref[0].astype(BF16)
    cv = cv_ref[0].astype(BF16)
    jc = _iota((1, n_cmp), 1)
    d_c = t - ((jc + 1) * NSA_BLOCK - 1)
    m_c = d_c >= 0
    d_cf = d_c.astype(F32)
    imp = jnp.zeros((tq, n_cmp), F32)
    o_c = []
    for h in range(NSA_HEADS):
        qh = q_ref[0, h].astype(BF16)
        p = _masked_softmax(_dot_nt(qh, ck) * NSA_SCALE - slopes[h] * d_cf, m_c)
        imp = imp + p
        o_c.append(_dot(p.astype(BF16), cv))
    if n_blk > n_cmp:
        imp = jnp.concatenate([imp, jnp.zeros((tq, n_blk - n_cmp), F32)], axis=-1)
    jb = _iota((1, n_blk), 1)
    cur = t >> NSA_BLOCK_SHIFT
    forced = (jb == 0) | (jb == cur) | (jb == cur - 1)
    score = jnp.where(jb <= cur, jnp.where(forced, NSA_FORCED, imp), -NSA_FORCED)
    sel, _ = _top_blocks(score, jb, min(NSA_TOPK, n_blk))
    sel_bf = jnp.where(sel, 1.0, 0.0).astype(BF16)
    band = NSA_WINDOW + tq
    wk = wk_ref[0, pl.ds(pl.multiple_of(q0, tq), band), :].astype(BF16)
    wv = wv_ref[0, pl.ds(pl.multiple_of(q0, tq), band), :].astype(BF16)
    wpos = q0 - NSA_WINDOW + _iota((1, band), 1)
    d_w = t - wpos
    m_w = (d_w >= 0) & (d_w < NSA_WINDOW) & (wpos >= 0)
    d_wf = d_w.astype(F32)

    def finish(kt):
        nb = kt // NSA_BLOCK
        expand = ((_iota((nb, kt), 1) >> NSA_BLOCK_SHIFT) == _iota((nb, kt), 0))
        key_sel = _dot(sel_bf[:, 0:nb], jnp.where(expand, 1.0, 0.0).astype(BF16)) > 0.5
        d_s = t - _iota((1, kt), 1)
        m_s = key_sel & (d_s >= 0)
        d_sf = d_s.astype(F32)
        sk = sk_ref[0, 0:kt, :].astype(BF16)
        sv = sv_ref[0, 0:kt, :].astype(BF16)
        for h in range(NSA_HEADS):
            qh = q_ref[0, h].astype(BF16)
            p_s = _masked_softmax(_dot_nt(qh, sk) * NSA_SCALE - slopes[h] * d_sf, m_s)
            o_s = _dot(p_s.astype(BF16), sv)
            p_w = _masked_softmax(_dot_nt(qh, wk) * NSA_SCALE - slopes[h] * d_wf, m_w)
            o_w = _dot(p_w.astype(BF16), wv)
            g = g_ref[0, h]
            o_ref[0, h] = g[:, 0:1] * o_c[h] + g[:, 1:2] * o_s + g[:, 2:3] * o_w

    _causal_segments(pl.program_id(1), T // tq, tq, finish)


def _prompt_nsa(q4, g4, ck, cv, sk, sv, wkp, wvp, tq=128):
    B, H, T, dh = q4.shape
    n_cmp = ck.shape[1]
    n_blk = -(-T // NSA_BLOCK)
    assert T % tq == 0 and T % NSA_BLOCK == 0
    per_b = lambda shape: pl.BlockSpec((1,) + shape, lambda b, i: (b,) + (0,) * len(shape))
    return pl.pallas_call(
        functools.partial(_pnsa_kernel, tq=tq, T=T, n_cmp=n_cmp, n_blk=n_blk),
        grid=(B, T // tq),
        in_specs=[pl.BlockSpec((1, H, tq, dh), lambda b, i: (b, 0, i, 0)),
                  pl.BlockSpec((1, H, tq, 3), lambda b, i: (b, 0, i, 0)),
                  per_b((n_cmp, dh)), per_b((n_cmp, dh)), per_b((T, dh)), per_b((T, dh)),
                  per_b((T + NSA_WINDOW, dh)), per_b((T + NSA_WINDOW, dh))],
        out_specs=pl.BlockSpec((1, H, tq, dh), lambda b, i: (b, 0, i, 0)),
        out_shape=jax.ShapeDtypeStruct((B, H, T, dh), F32),
        compiler_params=_params(("parallel", "parallel"), 48),
        name="prompt_nsa",
    )(q4, g4, ck, cv, sk, sv, wkp, wvp)


def _diff_lambda(lp_ref):
    lp = lp_ref[...]
    a = jnp.sum(lp[0:1] * lp[1:2], axis=-1, keepdims=True)
    b = jnp.sum(lp[2:3] * lp[3:4], axis=-1, keepdims=True)
    return jnp.exp(a) - jnp.exp(b)


def _pdiff_kernel(q_ref, k1_ref, k2_ref, v_ref, lp_ref, g_ref, o_ref, *, tq, T, lam_init):
    q0 = pl.program_id(1) * tq
    lam = _diff_lambda(lp_ref) + lam_init
    slopes = _alibi(DIFF_HEADS)

    def body(kt):
        dist = (q0 + _iota((tq, 1), 0)) - _iota((1, kt), 1)
        mask = dist >= 0
        distf = dist.astype(F32)
        ks = (k1_ref[0, 0:kt, :].astype(BF16), k2_ref[0, 0:kt, :].astype(BF16))
        v = v_ref[0, 0:kt, :].astype(BF16)
        for h in range(DIFF_HEADS):
            ps = []
            for c in range(2):
                qh = q_ref[0, 2 * h + c].astype(BF16)
                ps.append(_masked_softmax(_dot_nt(qh, ks[c]) * DIFF_SCALE - slopes[h] * distf, mask))
            w = ps[0] - lam * ps[1]
            o = _dot(w.astype(BF16), v)
            o_ref[0, :, h * DIFF_DV:(h + 1) * DIFF_DV] = _rms(o, g_ref[...]) * (1.0 - lam_init)

    _causal_segments(pl.program_id(1), T // tq, tq, body)


def _prompt_diff(q5, k1, k2, v, lam_p, subln, lam_init, tq=128):
    B, HC, T, dh = q5.shape
    per_b = lambda shape: pl.BlockSpec((1,) + shape, lambda b, i: (b,) + (0,) * len(shape))
    return pl.pallas_call(
        functools.partial(_pdiff_kernel, tq=tq, T=T, lam_init=lam_init),
        grid=(B, T // tq),
        in_specs=[pl.BlockSpec((1, HC, tq, dh), lambda b, i: (b, 0, i, 0)),
                  per_b((T, dh)), per_b((T, dh)), per_b((T, DIFF_DV)),
                  pl.BlockSpec((4, DIFF_DH), lambda b, i: (0, 0)),
                  pl.BlockSpec((1, DIFF_DV), lambda b, i: (0, 0))],
        out_specs=pl.BlockSpec((1, tq, DIFF_HEADS * DIFF_DV), lambda b, i: (b, i, 0)),
        out_shape=jax.ShapeDtypeStruct((B, T, DIFF_HEADS * DIFF_DV), F32),
        compiler_params=_params(("parallel", "parallel"), 48),
        name="prompt_diff",
    )(q5, k1, k2, v, lam_p, subln)


def _pmla_kernel(q_ref, k_ref, o_ref, *, tq, T):
    q0 = pl.program_id(1) * tq

    def body(kt):
        mask = ((q0 + _iota((tq, 1), 0)) - _iota((1, kt), 1)) >= 0
        k = k_ref[0, 0:kt, :].astype(BF16)
        lat = k[:, 0:MLA_KV_LORA]
        for h in range(MLA_HEADS):
            qh = q_ref[0, :, h * MLA_SLOT:(h + 1) * MLA_SLOT].astype(BF16)
            p = _masked_softmax(_dot_nt(qh, k) * MLA_SCALE, mask)
            o_ref[0, :, h * MLA_KV_LORA:(h + 1) * MLA_KV_LORA] = _dot(p.astype(BF16), lat)

    _causal_segments(pl.program_id(1), T // tq, tq, body)


def _prompt_mla(q_cat, k_cat, tq=128):
    B, T, _ = q_cat.shape
    return pl.pallas_call(
        functools.partial(_pmla_kernel, tq=tq, T=T),
        grid=(B, T // tq),
        in_specs=[pl.BlockSpec((1, tq, MLA_HEADS * MLA_SLOT), lambda b, i: (b, i, 0)),
                  pl.BlockSpec((1, T, MLA_SLOT), lambda b, i: (b, 0, 0))],
        out_specs=pl.BlockSpec((1, tq, MLA_HEADS * MLA_KV_LORA), lambda b, i: (b, i, 0)),
        out_shape=jax.ShapeDtypeStruct((B, T, MLA_HEADS * MLA_KV_LORA), F32),
        compiler_params=_params(("parallel", "parallel"), 48),
        name="prompt_mla",
    )(q_cat, k_cat)


def _page_copies(pt_ref, cache_ref, buf_ref, sem_ref, layer, seq, chunk, slot, *, n_pages, ppc, transposed):
    copies = []
    for j in range(ppc):
        page = pt_ref[seq * n_pages + chunk * ppc + j]
        if transposed:
            dst = buf_ref.at[slot, :, pl.ds(j * PAGE, PAGE)]
        else:
            dst = buf_ref.at[slot, pl.ds(j * PAGE, PAGE), :]
        copies.append(pltpu.make_async_copy(cache_ref.at[layer, page], dst, sem_ref.at[slot]))
    return copies


def _paged_kernel(pt_ref, q_ref, new_ref, *rest, kind, layer, n_pages, ppc, past_len, n_tok, lam_init):
    if kind == "diff":
        lp_ref, g_ref, cache_ref, o_ref, buf_ref, sem_ref, m_ref, l_ref, acc_ref = rest
    else:
        cache_ref, o_ref, buf_ref, sem_ref, m_ref, l_ref, acc_ref = rest
    b = pl.program_id(0)
    c = pl.program_id(1)
    nb = pl.num_programs(0)
    nc = pl.num_programs(1)
    step = b * nc + c
    slot = step % 2
    copies = functools.partial(_page_copies, pt_ref, cache_ref, buf_ref, sem_ref, layer,
                               n_pages=n_pages, ppc=ppc, transposed=kind == "mla")

    @pl.when(step == 0)
    def _():
        for cp in copies(b, c, slot):
            cp.start()

    @pl.when(step + 1 < nb * nc)
    def _():
        nxt = step + 1
        for cp in copies(nxt // nc, nxt % nc, 1 - slot):
            cp.start()

    @pl.when(c == 0)
    def _():
        m_ref[...] = jnp.full(m_ref.shape, NEG_INF, F32)
        l_ref[...] = jnp.zeros(l_ref.shape, F32)
        acc_ref[...] = jnp.zeros(acc_ref.shape, F32)

    rows = q_ref.shape[1]
    q = q_ref[0].astype(BF16)
    r = _iota((rows, 1), 0)
    if kind == "diff":
        tok = (r % (n_tok * DIFF_HEADS)) // DIFF_HEADS
        head = r % DIFF_HEADS
        slope = jnp.zeros((rows, 1), F32)
        for h, sl in enumerate(_alibi(DIFF_HEADS)):
            slope = jnp.where(head == h, sl, slope)
        scale = DIFF_SCALE
    else:
        tok = r // MLA_HEADS
        slope = None
        scale = MLA_SCALE
    pos = past_len + tok

    def update(s, pv, kpos, valid):
        s = s * scale
        dist = pos - kpos
        if slope is not None:
            s = s - slope * dist.astype(F32)
        if valid is not None:
            mask = (dist >= 0) & valid
            s = jnp.where(mask, s, NEG_INF)
        m_old = m_ref[...]
        m_new = jnp.maximum(m_old, jnp.max(s, axis=-1, keepdims=True))
        alpha = jnp.exp(m_old - m_new)
        p = jnp.exp(s - m_new)
        if valid is not None:
            p = jnp.where(mask, p, 0.0)
        l_ref[...] = alpha * l_ref[...] + jnp.sum(p, axis=-1, keepdims=True)
        acc_ref[...] = alpha * acc_ref[...] + pv(p.astype(BF16))
        m_ref[...] = m_new

    for cp in copies(b, c, slot):
        cp.wait()
    keys = ppc * PAGE
    kpos = c * keys + _iota((1, keys), 1)
    if kind == "diff":
        k = buf_ref[slot, :, 0:2 * DIFF_DH].astype(BF16)
        v = buf_ref[slot, :, 2 * DIFF_DH:].astype(BF16)
        update(_dot_nt(q, k), lambda p: _dot(p, v), kpos, None)
    else:
        lat_t = buf_ref[slot, 0:MLA_KV_LORA, :].astype(BF16)
        kr_t = buf_ref[slot, MLA_KV_LORA:, :].astype(BF16)
        s = _dot(q[:, 0:MLA_KV_LORA], lat_t) + _dot(q[:, MLA_KV_LORA:MLA_KV_LORA + MLA_ROPE], kr_t)
        update(s, lambda p: _dot_nt(p, lat_t), kpos, None)

    @pl.when(c == nc - 1)
    def _():
        nk = new_ref.shape[1]
        jn = _iota((1, nk), 1)
        kw = new_ref.shape[2] if kind == "mla" else 2 * DIFF_DH
        vlo = 0 if kind == "mla" else 2 * DIFF_DH
        kn = new_ref[0, :, 0:kw].astype(BF16)
        vn = new_ref[0, :, vlo:vlo + 128].astype(BF16)
        update(_dot_nt(q[:, 0:kw], kn), lambda p: _dot(p, vn), past_len + jn, jn < n_tok)
        o = acc_ref[...] * (1.0 / jnp.maximum(l_ref[...], 1e-30))
        if kind == "diff":
            half = rows // 2
            lam = _diff_lambda(lp_ref) + lam_init
            w = o[0:half] - lam * o[half:rows]
            o_ref[0] = _rms(w, g_ref[...]) * (1.0 - lam_init)
        else:
            o_ref[0] = o


def _paged_attention(kind, q, new_rows, cache, page_table, layer, extra=(), lam_init=0.0, ppc=32):
    B, rows, qw = q.shape
    n_pages = page_table.shape[1]
    ppc = min(ppc, n_pages)
    if kind == "mla":
        assert n_pages % ppc == 0 and cache.shape[3] == PAGE
        buf_shape = (2, cache.shape[2], ppc * PAGE)
    else:
        assert n_pages % ppc == 0 and cache.shape[2] == PAGE
        buf_shape = (2, ppc * PAGE, cache.shape[3])
    n_tok = rows // (2 * DIFF_HEADS if kind == "diff" else MLA_HEADS)
    out_rows = rows // 2 if kind == "diff" else rows
    nc = n_pages // ppc
    in_specs = [pl.BlockSpec((1, rows, qw), lambda b, c, pt: (b, 0, 0)),
                pl.BlockSpec((1,) + new_rows.shape[1:], lambda b, c, pt: (b, 0, 0))]
    in_specs += [pl.BlockSpec(e.shape, lambda b, c, pt: (0, 0)) for e in extra]
    in_specs.append(pl.BlockSpec(memory_space=pl.ANY))
    grid_spec = pltpu.PrefetchScalarGridSpec(
        num_scalar_prefetch=1,
        grid=(B, nc),
        in_specs=in_specs,
        out_specs=pl.BlockSpec((1, out_rows, 128), lambda b, c, pt: (b, 0, 0)),
        scratch_shapes=[pltpu.VMEM(buf_shape, F32),
                        pltpu.SemaphoreType.DMA((2,)),
                        pltpu.VMEM((rows, 1), F32), pltpu.VMEM((rows, 1), F32), pltpu.VMEM((rows, 128), F32)],
    )
    return pl.pallas_call(
        functools.partial(_paged_kernel, kind=kind, layer=layer, n_pages=n_pages, ppc=ppc,
                          past_len=n_pages * PAGE, n_tok=n_tok, lam_init=lam_init),
        grid_spec=grid_spec,
        out_shape=jax.ShapeDtypeStruct((B, out_rows, 128), F32),
        compiler_params=_params(("arbitrary", "arbitrary"), 48),
        name="paged_" + kind,
    )(page_table.reshape(-1), q, new_rows, *extra, cache)


def _snsa_a_copies(pt_ref, cache_ref, buf_ref, sem_ref, layer, seq, slot, *, n_pages):
    copies = []
    for p in range(n_pages):
        page = pt_ref[seq * n_pages + p]
        copies.append(pltpu.make_async_copy(
            cache_ref.at[layer, page, 0:2],
            buf_ref.at[slot, :, :, p, :],
            sem_ref.at[slot]))
    return copies


def _snsa_a_kernel(pt_ref, q_ref, win_ref, w1_ref, b1_ref, w2k_ref, w2v_ref, cache_ref, oc_ref, ow_ref, top_ref,
                   buf_ref, sem_ref, *, layer, n_pages, past_len, n_tok):
    b = pl.program_id(0)
    nb = pl.num_programs(0)
    slot = b % 2
    copies = functools.partial(_snsa_a_copies, pt_ref, cache_ref, buf_ref, sem_ref, layer, n_pages=n_pages)

    @pl.when(b == 0)
    def _():
        for cp in copies(b, slot):
            cp.start()

    @pl.when(b + 1 < nb)
    def _():
        for cp in copies(b + 1, 1 - slot):
            cp.start()

    rows = q_ref.shape[1]
    q = q_ref[0].astype(BF16)
    r = _iota((rows, 1), 0)
    head = r % NSA_HEADS
    pos = past_len + r // NSA_HEADS
    slope = jnp.zeros((rows, 1), F32)
    for h, sl in enumerate(_alibi(NSA_HEADS)):
        slope = jnp.where(head == h, sl, slope)

    nw = win_ref.shape[1]
    wkv = win_ref[0].astype(BF16)
    wpos = past_len + n_tok - nw + _iota((1, nw), 1)
    d_w = pos - wpos
    m_w = (d_w >= 0) & (d_w < NSA_WINDOW)
    p_w = _masked_softmax(_dot_nt(q, wkv) * NSA_SCALE - slope * d_w.astype(F32), m_w)
    ow_ref[0] = _dot(p_w.astype(BF16), wkv)

    for cp in copies(b, slot):
        cp.wait()

    hid = []
    for c in range(2):
        acc = jnp.zeros((n_pages, 2 * NSA_CMP_HID), F32)
        for d in range(NSA_DH):
            acc = acc + _dot(buf_ref[slot, c, d].astype(BF16), w1_ref[c, d])
        hid.append(jax.nn.gelu(acc + b1_ref[c]).astype(BF16))
    halves = [_dot(hid[0][:, i * NSA_CMP_HID:(i + 1) * NSA_CMP_HID], w2k_ref[...])
              + _dot(hid[1][:, i * NSA_CMP_HID:(i + 1) * NSA_CMP_HID], w2v_ref[...]) for i in range(2)]
    ckv = jnp.concatenate(halves, axis=0).astype(BF16)
    n_cmp = 2 * n_pages
    col = _iota((1, n_cmp), 1)
    jc = 2 * (col % n_pages) + col // n_pages
    d_c = pos - ((jc + 1) * NSA_BLOCK - 1)
    p_c = _masked_softmax(_dot_nt(q, ckv) * NSA_SCALE - slope * d_c.astype(F32), d_c >= 0)
    oc_ref[0] = _dot(p_c.astype(BF16), ckv)

    imp = jnp.sum(p_c.reshape(n_tok, NSA_HEADS, n_cmp), axis=1)
    n_blk = -(-(past_len + n_tok) // NSA_BLOCK)
    pad = (-(n_cmp + 1)) % LANES + 1
    imp = jnp.concatenate([imp, jnp.zeros((n_tok, pad), F32)], axis=-1)
    jb = jnp.concatenate([jc, n_cmp + _iota((1, pad), 1)], axis=-1)
    cur = (past_len + _iota((n_tok, 1), 0)) >> NSA_BLOCK_SHIFT
    forced = (jb == 0) | (jb == cur) | (jb == cur - 1)
    score = jnp.where(jb <= cur, jnp.where(forced, NSA_FORCED, imp), -NSA_FORCED)
    score = jnp.where(jb < n_blk, score, -3e38)
    _, picks = _top_blocks(score, jb, NSA_TOPK)
    lane = _iota((n_tok, LANES), 1)
    top = jnp.zeros((n_tok, LANES), I32)
    for k, idx in enumerate(picks):
        top = jnp.where(lane == k, idx, top)
    top_ref[0] = top


def _sample_nsa_a(q, win_cat, w1t, b1t, w2k, w2v, cache_t, page_table, layer, n_tok):
    B, rows, _ = q.shape
    n_pages = page_table.shape[1]
    nw = win_cat.shape[1]
    const = lambda shape: pl.BlockSpec(shape, lambda b, pt: (0,) * len(shape))
    grid_spec = pltpu.PrefetchScalarGridSpec(
        num_scalar_prefetch=1,
        grid=(B,),
        in_specs=[pl.BlockSpec((1, rows, 128), lambda b, pt: (b, 0, 0)),
                  pl.BlockSpec((1, nw, 128), lambda b, pt: (b, 0, 0)),
                  const(w1t.shape), const(b1t.shape), const(w2k.shape), const(w2v.shape),
                  pl.BlockSpec(memory_space=pl.ANY)],
        out_specs=[pl.BlockSpec((1, rows, 128), lambda b, pt: (b, 0, 0)),
                   pl.BlockSpec((1, rows, 128), lambda b, pt: (b, 0, 0)),
                   pl.BlockSpec((1, n_tok, LANES), lambda b, pt: (b, 0, 0))],
        scratch_shapes=[pltpu.VMEM((2, 2, NSA_DH, n_pages, PAGE), F32), pltpu.SemaphoreType.DMA((2,))],
    )
    return pl.pallas_call(
        functools.partial(_snsa_a_kernel, layer=layer, n_pages=n_pages, past_len=n_pages * PAGE, n_tok=n_tok),
        grid_spec=grid_spec,
        out_shape=[jax.ShapeDtypeStruct((B, rows, 128), F32), jax.ShapeDtypeStruct((B, rows, 128), F32),
                   jax.ShapeDtypeStruct((B, n_tok, LANES), I32)],
        compiler_params=_params(("arbitrary",), 56),
        name="sample_nsa_a",
    )(page_table.reshape(-1), q, win_cat, w1t, b1t, w2k, w2v, cache_t)


def _snsa_b_copies(pt_ref, top_ref, cache_ref, buf_ref, sem_ref, layer, seq, slot, *, n_pages, n_tok):
    copies = []
    n_cmp = 2 * n_pages
    for t in range(n_tok):
        for k in range(NSA_TOPK):
            blk = top_ref[(seq * n_tok + t) * NSA_TOPK + k]
            blk = jnp.where(blk < n_cmp, blk, 0)
            page = pt_ref[seq * n_pages + (blk >> 1)]
            copies.append(pltpu.make_async_copy(
                cache_ref.at[layer, page, 2:4],
                buf_ref.at[slot, t, :, :, pl.ds(k * PAGE, PAGE)],
                sem_ref.at[slot]))
    return copies


def _snsa_b_kernel(pt_ref, top_ref, q_ref, nk_ref, nv_ref, oc_ref, ow_ref, g_ref, cache_ref, o_ref, buf_ref, sem_ref,
                   *, layer, n_pages, past_len, n_tok):
    b = pl.program_id(0)
    nb = pl.num_programs(0)
    slot = b % 2
    copies = functools.partial(_snsa_b_copies, pt_ref, top_ref, cache_ref, buf_ref, sem_ref, layer,
                               n_pages=n_pages, n_tok=n_tok)

    @pl.when(b == 0)
    def _():
        for cp in copies(b, slot):
            cp.start()

    @pl.when(b + 1 < nb)
    def _():
        for cp in copies(b + 1, 1 - slot):
            cp.start()

    for cp in copies(b, slot):
        cp.wait()

    n_cmp = 2 * n_pages
    keys = NSA_TOPK * PAGE
    lane = _iota((1, keys), 1)
    slot_of = lane // PAGE
    within = lane % PAGE
    jn = _iota((1, nk_ref.shape[2]), 1)
    new_k = nk_ref[0].astype(BF16)
    new_v = nv_ref[0].astype(BF16)
    slopes = jnp.zeros((NSA_HEADS, 1), F32)
    hrow = _iota((NSA_HEADS, 1), 0)
    for h, sl in enumerate(_alibi(NSA_HEADS)):
        slopes = jnp.where(hrow == h, sl, slopes)
    for t in range(n_tok):
        blk_of = jnp.zeros((1, keys), I32)
        has_new = jnp.int32(0)
        for k in range(NSA_TOPK):
            blk = top_ref[(b * n_tok + t) * NSA_TOPK + k]
            blk_of = jnp.where(slot_of == k, blk, blk_of)
            has_new = jnp.maximum(has_new, (blk == n_cmp).astype(I32))
        pos = past_len + t
        q = q_ref[0, t].astype(BF16)
        k_t = buf_ref[slot, t, 0].astype(BF16)
        v_t = buf_ref[slot, t, 1].astype(BF16)
        d_g = pos - ((blk_of >> 1) * PAGE + within)
        m_g = (d_g >= 0) & (blk_of < n_cmp) & ((within >> NSA_BLOCK_SHIFT) == (blk_of & 1))
        s_g = jnp.where(m_g, _dot(q, k_t) * NSA_SCALE - slopes * d_g.astype(F32), NEG_INF)
        d_n = pos - (past_len + jn)
        m_n = (d_n >= 0) & (jn < has_new * n_tok)
        s_n = jnp.where(m_n, _dot(q, new_k) * NSA_SCALE - slopes * d_n.astype(F32), NEG_INF)
        m = jnp.maximum(jnp.max(s_g, axis=-1, keepdims=True), jnp.max(s_n, axis=-1, keepdims=True))
        e_g = jnp.where(m_g, jnp.exp(s_g - m), 0.0)
        e_n = jnp.where(m_n, jnp.exp(s_n - m), 0.0)
        den = jnp.sum(e_g, axis=-1, keepdims=True) + jnp.sum(e_n, axis=-1, keepdims=True)
        o_s = (_dot_nt(e_g.astype(BF16), v_t) + _dot_nt(e_n.astype(BF16), new_v)) * (1.0 / jnp.maximum(den, 1e-30))
        g = g_ref[0, t]
        o_ref[0, t] = g[:, 0:1] * oc_ref[0, t] + g[:, 1:2] * o_s + g[:, 2:3] * ow_ref[0, t]


def _sample_nsa_b(q, new_k, new_v, o_c, o_w, gates, top, cache_t, page_table, layer):
    B, n_tok, H, dh = q.shape
    n_pages = page_table.shape[1]
    per_b = lambda shape: pl.BlockSpec((1,) + shape, lambda b, pt, tp: (b,) + (0,) * len(shape))
    grid_spec = pltpu.PrefetchScalarGridSpec(
        num_scalar_prefetch=2,
        grid=(B,),
        in_specs=[per_b((n_tok, H, dh)), per_b(new_k.shape[1:]), per_b(new_v.shape[1:]), per_b((n_tok, H, dh)),
                  per_b((n_tok, H, dh)), per_b((n_tok, H, 3)), pl.BlockSpec(memory_space=pl.ANY)],
        out_specs=per_b((n_tok, H, dh)),
        scratch_shapes=[pltpu.VMEM((2, n_tok, 2, dh, NSA_TOPK * PAGE), F32), pltpu.SemaphoreType.DMA((2,))],
    )
    return pl.pallas_call(
        functools.partial(_snsa_b_kernel, layer=layer, n_pages=n_pages, past_len=n_pages * PAGE, n_tok=n_tok),
        grid_spec=grid_spec,
        out_shape=jax.ShapeDtypeStruct((B, n_tok, H, dh), F32),
        compiler_params=_params(("arbitrary",), 56),
        name="sample_nsa_b",
    )(page_table.reshape(-1), top.reshape(-1), q, new_k, new_v, o_c, o_w, gates, cache_t)


def _out_ln_kernel(a_ref, w_ref, x_ref, g_ref, b_ref, y_ref, yb_ref, *, alpha):
    mix = _dot(a_ref[...].astype(BF16), w_ref[...])
    y = _layer_norm(alpha * x_ref[...] + mix, g_ref[...], b_ref[...])
    y_ref[...] = y
    yb_ref[...] = y.astype(BF16)


def _out_ln(a, w_bf16, x, g, b, alpha, tm=256):
    N, K = a.shape
    tm = _tile(N, tm)
    D = w_bf16.shape[1]
    row = lambda w: pl.BlockSpec((tm, w), lambda i: (i, 0))
    const = lambda shape: pl.BlockSpec(shape, lambda i: (0, 0))
    return pl.pallas_call(
        functools.partial(_out_ln_kernel, alpha=alpha),
        grid=(N // tm,),
        in_specs=[row(K), const((K, D)), row(D), const((1, D)), const((1, D))],
        out_specs=[row(D), row(D)],
        out_shape=[jax.ShapeDtypeStruct((N, D), F32), jax.ShapeDtypeStruct((N, D), BF16)],
        compiler_params=_params(("parallel",), 48),
        name="out_ln",
    )(a, w_bf16, x, g, b)


def _split_bf16(v):
    hi = v.astype(BF16)
    lo = (v - hi.astype(F32)).astype(BF16)
    return hi, lo


def _route_kernel(x_ref, w_ref, b_ref, gate_ref):
    xh, xl = _split_bf16(x_ref[...])
    wh, wl = _split_bf16(w_ref[...])
    logits = _dot(xh, wh) + (_dot(xh, wl) + _dot(xl, wh)) + b_ref[...]
    tm = logits.shape[0]
    lane = _iota((tm, LANES), 1)
    is_grp = (lane >= N_EXPERTS) & (lane < N_EXPERTS + N_GROUPS)
    lg = jnp.where(is_grp, logits, NEG_INF)
    mg = jnp.max(lg, axis=-1, keepdims=True)
    eg = jnp.where(is_grp, jnp.exp(lg - mg), 0.0)
    pg = eg / jnp.sum(eg, axis=-1, keepdims=True)
    g_w = jnp.max(pg, axis=-1, keepdims=True)
    g_idx = jnp.min(jnp.where(is_grp & (pg == g_w), lane, 1 << 30), axis=-1, keepdims=True) - N_EXPERTS
    in_grp = (lane >= g_idx * EXP_PER_GROUP) & (lane < (g_idx + 1) * EXP_PER_GROUP)
    le = jnp.where(in_grp, logits, NEG_INF)
    me = jnp.max(le, axis=-1, keepdims=True)
    ee = jnp.where(in_grp, jnp.exp(le - me), 0.0)
    pe = ee / jnp.sum(ee, axis=-1, keepdims=True)
    pe = jnp.where(in_grp, pe, -1.0)
    e1 = jnp.max(pe, axis=-1, keepdims=True)
    i1 = jnp.min(jnp.where(pe == e1, lane, 1 << 30), axis=-1, keepdims=True)
    pe2 = jnp.where(lane == i1, -1.0, pe)
    e2 = jnp.max(pe2, axis=-1, keepdims=True)
    i2 = jnp.min(jnp.where(pe2 == e2, lane, 1 << 30), axis=-1, keepdims=True)
    tot = e1 + e2
    gate_ref[...] = jnp.where(lane == i1, g_w * e1 / tot, jnp.where(lane == i2, g_w * e2 / tot, 0.0))


def _route(x, w_route, b_route, tm=512):
    N, D = x.shape
    tm = _tile(N, tm)
    return pl.pallas_call(
        _route_kernel,
        grid=(N // tm,),
        in_specs=[pl.BlockSpec((tm, D), lambda i: (i, 0)), pl.BlockSpec((D, LANES), lambda i: (0, 0)),
                  pl.BlockSpec((1, LANES), lambda i: (0, 0))],
        out_specs=pl.BlockSpec((tm, LANES), lambda i: (i, 0)),
        out_shape=jax.ShapeDtypeStruct((N, LANES), F32),
        compiler_params=_params(("parallel",), 40),
        name="route",
    )(x, w_route, b_route)


MOE_ROW_CHUNKS = 2


def _moe_kernel(x_ref, gate_ref, w1_ref, w3_ref, w2_ref, o_ref):
    e = pl.program_id(1)
    w1 = w1_ref[0].astype(BF16)
    w3 = w3_ref[0].astype(BF16)
    w2 = w2_ref[0].astype(BF16)
    rc = x_ref.shape[0] // MOE_ROW_CHUNKS
    for r in range(MOE_ROW_CHUNKS):
        rows = pl.ds(r * rc, rc)
        x = x_ref[rows, :]
        gate = gate_ref[rows, :]
        g = jnp.sum(jnp.where(_iota(gate.shape, 1) == e, gate, 0.0), axis=-1, keepdims=True)
        h = (jax.nn.silu(_dot(x, w1)) * _dot(x, w3)) * g
        y = _dot(h.astype(BF16), w2)

        @pl.when(e == 0)
        def _():
            o_ref[rows, :] = y

        @pl.when(e > 0)
        def _():
            o_ref[rows, :] += y


def _moe(xb, gate, w1, w3, w2, tm=1024):
    N, D = xb.shape
    tm = _tile(N, tm)
    assert tm % (8 * MOE_ROW_CHUNKS) == 0
    E, _, Hd = w1.shape
    return pl.pallas_call(
        _moe_kernel,
        grid=(N // tm, E),
        in_specs=[pl.BlockSpec((tm, D), lambda i, e: (i, 0)), pl.BlockSpec((tm, LANES), lambda i, e: (i, 0)),
                  pl.BlockSpec((1, D, Hd), lambda i, e: (e, 0, 0)), pl.BlockSpec((1, D, Hd), lambda i, e: (e, 0, 0)),
                  pl.BlockSpec((1, Hd, D), lambda i, e: (e, 0, 0))],
        out_specs=pl.BlockSpec((tm, D), lambda i, e: (i, 0)),
        out_shape=jax.ShapeDtypeStruct((N, D), F32),
        compiler_params=_params(("parallel", "arbitrary"), 56),
        name="moe",
    )(xb, gate, w1, w3, w2)


def _res_ln_kernel(x_ref, y_ref, g_ref, b_ref, o_ref, *, alpha):
    o_ref[...] = _layer_norm(alpha * x_ref[...] + y_ref[...], g_ref[...], b_ref[...])


def _res_ln(x, y, g, b, alpha, tm=512):
    N, D = x.shape
    tm = _tile(N, tm)
    row = pl.BlockSpec((tm, D), lambda i: (i, 0))
    const = pl.BlockSpec((1, D), lambda i: (0, 0))
    return pl.pallas_call(
        functools.partial(_res_ln_kernel, alpha=alpha),
        grid=(N // tm,),
        in_specs=[row, row, const, const],
        out_specs=row,
        out_shape=jax.ShapeDtypeStruct((N, D), F32),
        compiler_params=_params(("parallel",), 40),
        name="res_ln",
    )(x, y, g, b)


def _layer_weights(w_in, pe, w1, w2, w_uq, w_uk, w_uv, wg, bg, we, be):
    D = w_in.shape[0]
    o = np.cumsum([0, 512, 64, 64, 64, 64, 64, 64, 24, 512, 128, 128, 384, 128, 32])
    seg = lambda i: w_in[:, o[i]:o[i + 1]]
    kr = seg(13)
    half = MLA_ROPE // 2
    w_in_p = jnp.concatenate(
        [seg(0), seg(1), seg(2), seg(3), seg(4), seg(5), seg(6), seg(8), seg(9), seg(10), seg(11), seg(12), kr,
         jnp.concatenate([kr[:, half:], kr[:, :half]], axis=1), seg(7),
         jnp.zeros((D, H_COLS - C_GATE - 24), F32)], axis=1)
    w1t = []
    for c in range(2):
        a = w1[c].reshape(NSA_BLOCK, NSA_DH, NSA_CMP_HID).transpose(1, 0, 2)
        z = jnp.zeros_like(a)
        w1t.append(jnp.concatenate([jnp.concatenate([a, z], axis=2), jnp.concatenate([z, a], axis=2)], axis=1))
    w1t = jnp.stack(w1t)
    z2 = jnp.zeros_like(w2[0])
    w2k = jnp.concatenate([w2[0], z2], axis=1)
    w2v = jnp.concatenate([z2, w2[1]], axis=1)
    uq = w_uq.reshape(MLA_Q_LORA, MLA_HEADS, MLA_NOPE + MLA_ROPE)
    uq_r = uq[:, :, MLA_NOPE:]
    uq_rs = jnp.concatenate([uq_r[:, :, half:], uq_r[:, :, :half]], axis=2)
    w_uq_p = jnp.concatenate([uq[:, :, :MLA_NOPE].reshape(MLA_Q_LORA, -1), uq_r.reshape(MLA_Q_LORA, -1),
                              uq_rs.reshape(MLA_Q_LORA, -1)], axis=1)
    eye_h = jnp.eye(MLA_HEADS, dtype=F32)
    top = w_uk.transpose(1, 2, 0)[:, :, None, :] * eye_h[:, None, :, None]
    top = jnp.pad(top, ((0, 0), (0, 0), (0, 0), (0, MLA_SLOT - MLA_KV_LORA)))
    bot = eye_h[:, None, :, None] * jnp.eye(MLA_ROPE, dtype=F32)[None, :, None, :]
    bot = jnp.pad(bot, ((0, 0), (0, 0), (0, 0), (MLA_KV_LORA, MLA_SLOT - MLA_KV_LORA - MLA_ROPE)))
    w_abs = jnp.concatenate([top.reshape(MLA_HEADS * MLA_NOPE, -1), bot.reshape(MLA_HEADS * MLA_ROPE, -1)], axis=0)
    w_uv_bd = (w_uv.transpose(1, 0, 2)[:, :, None, :] * eye_h[:, None, :, None]).reshape(
        MLA_HEADS * MLA_KV_LORA, MLA_HEADS * MLA_DV)
    w_route = jnp.concatenate([we, wg, jnp.zeros((D, LANES - N_EXPERTS - N_GROUPS), F32)], axis=1)
    b_route = jnp.concatenate([be, bg, jnp.zeros((LANES - N_EXPERTS - N_GROUPS,), F32)])[None, :]
    return dict(w_in_p=w_in_p, w1t=w1t.astype(BF16), w2k=w2k.astype(BF16), w2v=w2v.astype(BF16), w_uq_p=w_uq_p,
                w_abs=w_abs.astype(BF16), w_uv_bd=w_uv_bd, w_route=w_route, b_route=b_route)


def _rope_tables(pos):
    half = MLA_ROPE // 2
    inv = ROPE_BASE ** (-jnp.arange(half, dtype=F32) / half)
    ang = pos.astype(F32)[:, None] * inv[None, :]
    cos, sin = jnp.cos(ang), jnp.sin(ang)
    cos32 = jnp.concatenate([cos, cos], axis=1)
    sin32 = jnp.concatenate([-sin, sin], axis=1)
    return cos32, sin32, jnp.tile(cos32, (1, MLA_HEADS)), jnp.tile(sin32, (1, MLA_HEADS))


def _layer(x_all, tables, lw, wl, layer, depth, Bp, T, Bs, Ts, cache_nsa_t, cache_diff, cache_mla_t, win_state,
           page_table):
    cos32, sin32, cos256, sin256 = tables
    Np = Bp * T
    alpha = (2 * depth) ** 0.25
    lam_init = 0.8 - 0.6 * math.exp(-0.3 * layer)
    n_pages = page_table.shape[1]
    past_len = n_pages * PAGE

    h = _mm(x_all, wl["w_in_p"], 1024, 256)
    cqn, lat, krope, gates = _prep(h, cos32, sin32, lw["mla_q_norm"][None, :], lw["mla_kv_norm"][None, :])
    q1 = _mm(cqn, wl["w_uq_p"], 1024, 1024)
    q_cat = _mla_q(q1, cos256, sin256, wl["w_abs"])

    nsa_rows = h[:, C_NSA_ROWS:C_WIN]
    win_rows = h[:, C_WIN:C_DIFF_Q]
    diff_rows = h[:, C_DIFF_KV:C_MLA_CQ]
    mla_rows = jnp.concatenate([lat, krope], axis=1)

    hp = h[:Np].reshape(Bp, T, H_COLS)
    rows_p = nsa_rows[:Np].reshape(Bp, T, 4, NSA_DH)
    n_cmp = T // NSA_BLOCK
    pe = lw["nsa_cmp_pe"].reshape(2, 1, NSA_BLOCK * NSA_DH)
    cmp_p = []
    for i in range(2):
        xin = rows_p[:, :n_cmp * NSA_BLOCK, i].reshape(Bp * n_cmp, NSA_BLOCK * NSA_DH)
        hid = _mm(xin, lw["nsa_cmp_w1"][i], 128, 128, a_bias=pe[i], act="gelu")
        cmp_p.append(_mm(hid, lw["nsa_cmp_w2"][i], 128, NSA_DH).reshape(Bp, n_cmp, NSA_DH))
    q4 = hp[:, :, C_NSA_Q:C_NSA_ROWS].reshape(Bp, T, NSA_HEADS, NSA_DH).transpose(0, 2, 1, 3)
    g4 = gates[:Np].reshape(Bp, T, NSA_HEADS, 3).transpose(0, 2, 1, 3)
    win_p = win_rows[:Np].reshape(Bp, T, 2, NSA_DH)
    win_pad = jnp.pad(win_p, ((0, 0), (NSA_WINDOW, 0), (0, 0), (0, 0)))
    o_nsa_p = _prompt_nsa(q4, g4, cmp_p[0], cmp_p[1], rows_p[:, :, 2], rows_p[:, :, 3],
                          win_pad[:, :, 0], win_pad[:, :, 1])
    o_nsa_p = o_nsa_p.transpose(0, 2, 1, 3).reshape(Np, NSA_HEADS * NSA_DH)
    q5 = hp[:, :, C_DIFF_Q:C_DIFF_KV].reshape(Bp, T, 2 * DIFF_HEADS, DIFF_DH).transpose(0, 2, 1, 3)
    dkv_p = diff_rows[:Np].reshape(Bp, T, 2 * DIFF_DH + DIFF_DV)
    o_diff_p = _prompt_diff(q5, dkv_p[:, :, 0:DIFF_DH], dkv_p[:, :, DIFF_DH:2 * DIFF_DH], dkv_p[:, :, 2 * DIFF_DH:],
                            lw["diff_lambda"], lw["diff_subln"][None, :], lam_init).reshape(Np, -1)
    k_cat_p = jnp.pad(mla_rows[:Np], ((0, 0), (0, MLA_SLOT - mla_rows.shape[1]))).reshape(Bp, T, MLA_SLOT)
    o_lat_p = _prompt_mla(q_cat[:Np].reshape(Bp, T, -1), k_cat_p).reshape(Np, -1)

    hs = h[Np:].reshape(Bs, Ts, H_COLS)
    pad_new = lambda a: jnp.pad(a, ((0, 0), (0, PAGE - Ts), (0, 0)))
    qd = hs[:, :, C_DIFF_Q:C_DIFF_KV].reshape(Bs, Ts, DIFF_HEADS, 2, DIFF_DH).transpose(0, 3, 1, 2, 4)
    qd = qd.reshape(Bs, 2, Ts * DIFF_HEADS, DIFF_DH)
    zq = jnp.zeros_like(qd[:, 0])
    q_diff = jnp.concatenate([jnp.concatenate([qd[:, 0], zq], axis=-1), jnp.concatenate([zq, qd[:, 1]], axis=-1)], axis=1)
    o_diff_s = _paged_attention("diff", q_diff, pad_new(diff_rows[Np:].reshape(Bs, Ts, -1)), cache_diff, page_table,
                                layer, extra=(lw["diff_lambda"], lw["diff_subln"][None, :]), lam_init=lam_init)
    o_diff_s = o_diff_s.reshape(Bs * Ts, DIFF_HEADS * DIFF_DV)
    q_mla = q_cat[Np:].reshape(Bs, Ts * MLA_HEADS, MLA_SLOT)
    o_lat_s = _paged_attention("mla", q_mla, pad_new(mla_rows[Np:].reshape(Bs, Ts, -1)), cache_mla_t, page_table,
                               layer)
    o_lat_s = o_lat_s.reshape(Bs * Ts, MLA_HEADS * MLA_KV_LORA)
    qn = hs[:, :, C_NSA_Q:C_NSA_ROWS].reshape(Bs, Ts, NSA_HEADS, NSA_DH)
    qn_pad = jnp.concatenate([qn, jnp.zeros_like(qn)], axis=-1)
    win_new = win_rows[Np:].reshape(Bs, Ts, 2 * NSA_DH)
    win_cat = jnp.concatenate([win_state.reshape(Bs, -1, 2 * NSA_DH), win_new], axis=1)
    pe_bias = [_mm(jnp.broadcast_to(pe[i], (8, NSA_BLOCK * NSA_DH)), lw["nsa_cmp_w1"][i], 8, 128)[0:1] for i in range(2)]
    b1t = jnp.stack([jnp.concatenate([pb, pb], axis=1) for pb in pe_bias])
    o_c, o_w, top = _sample_nsa_a(qn_pad.reshape(Bs, Ts * NSA_HEADS, 128), win_cat, wl["w1t"], b1t, wl["w2k"],
                                  wl["w2v"], cache_nsa_t, page_table, layer, Ts)
    top = top[:, :, :NSA_TOPK]
    new_sel = pad_new(nsa_rows[Np:].reshape(Bs, Ts, 4 * NSA_DH)[:, :, 2 * NSA_DH:]).transpose(0, 2, 1)
    g_s = gates[Np:].reshape(Bs, Ts, NSA_HEADS, 3)
    o_nsa_s = _sample_nsa_b(qn, new_sel[:, :NSA_DH], new_sel[:, NSA_DH:],
                            o_c[..., NSA_DH:].reshape(Bs, Ts, NSA_HEADS, NSA_DH),
                            o_w[..., NSA_DH:].reshape(Bs, Ts, NSA_HEADS, NSA_DH),
                            g_s, top, cache_nsa_t, page_table, layer)
    o_nsa_s = o_nsa_s.reshape(Bs * Ts, NSA_HEADS * NSA_DH)

    o_lat = jnp.concatenate([o_lat_p, o_lat_s], axis=0)
    o_mla = _mm(o_lat, wl["w_uv_bd"], 1024, 1024)
    mix = jnp.concatenate([jnp.concatenate([o_nsa_p, o_nsa_s], axis=0),
                           jnp.concatenate([o_diff_p, o_diff_s], axis=0), o_mla], axis=1)
    x1, x1b = _out_ln(mix, lw["w_out"].astype(BF16), x_all, lw["ln1_g"][None, :], lw["ln1_b"][None, :], alpha)
    gate = _route(x1, wl["w_route"], wl["b_route"])
    y = _moe(x1b, gate, lw["moe_w1"], lw["moe_w3"], lw["moe_w2"])
    x2 = _res_ln(x1, y, lw["ln2_g"][None, :], lw["ln2_b"][None, :], alpha)

    new_win_p = win_p[:, T - min(NSA_WINDOW, T):]
    new_win_s = win_cat[:, win_cat.shape[1] - min(NSA_WINDOW, win_cat.shape[1]):].reshape(Bs, -1, 2, NSA_DH)
    rows = dict(
        nsa_p=rows_p, nsa_s=nsa_rows[Np:].reshape(Bs, Ts, 4, NSA_DH),
        diff_p=dkv_p, diff_s=diff_rows[Np:].reshape(Bs, Ts, -1),
        mla_p=mla_rows[:Np].reshape(Bp, T, -1), mla_s=mla_rows[Np:].reshape(Bs, Ts, -1),
        win_p=new_win_p, win_s=new_win_s)
    return x2, rows


def kernel(x_prompt, x_sample, cache_nsa, cache_diff, cache_mla, state_nsa_win, page_table, w_in, nsa_cmp_pe, nsa_cmp_w1, nsa_cmp_w2, diff_lambda, diff_subln, mla_q_norm, mla_w_uq, mla_kv_norm, mla_w_uk, mla_w_uv, w_out, ln1_g, ln1_b, moe_w_group, moe_b_group, moe_w_expert, moe_b_expert, moe_w1, moe_w3, moe_w2, ln2_g, ln2_b):
    Bp, T, D = x_prompt.shape
    Bs, Ts, _ = x_sample.shape
    depth = w_in.shape[0]
    past_len = page_table.shape[1] * PAGE
    pos = jnp.concatenate([jnp.tile(jnp.arange(T, dtype=I32), Bp),
                           jnp.tile(past_len + jnp.arange(Ts, dtype=I32), Bs)])
    tables = _rope_tables(pos)
    x_all = jnp.concatenate([x_prompt.reshape(Bp * T, D), x_sample.reshape(Bs * Ts, D)], axis=0)
    cache_nsa_t = cache_nsa.transpose(0, 1, 3, 4, 2)
    cache_mla_t = cache_mla.transpose(0, 1, 3, 2)
    outs = []
    for l in range(depth):
        lw = dict(nsa_cmp_pe=nsa_cmp_pe[l], nsa_cmp_w1=nsa_cmp_w1[l], nsa_cmp_w2=nsa_cmp_w2[l],
                  diff_lambda=diff_lambda[l], diff_subln=diff_subln[l], mla_q_norm=mla_q_norm[l],
                  mla_kv_norm=mla_kv_norm[l], w_out=w_out[l], ln1_g=ln1_g[l], ln1_b=ln1_b[l],
                  moe_w1=moe_w1[l], moe_w3=moe_w3[l], moe_w2=moe_w2[l], ln2_g=ln2_g[l], ln2_b=ln2_b[l])
        wl = _layer_weights(w_in[l], nsa_cmp_pe[l], nsa_cmp_w1[l], nsa_cmp_w2[l], mla_w_uq[l], mla_w_uk[l],
                            mla_w_uv[l], moe_w_group[l], moe_b_group[l], moe_w_expert[l], moe_b_expert[l])
        x_all, rows = _layer(x_all, tables, lw, wl, l, depth, Bp, T, Bs, Ts, cache_nsa_t, cache_diff, cache_mla_t,
                             state_nsa_win[l], page_table)
        outs.append(rows)
    Np = Bp * T
    st = lambda k: jnp.stack([r[k] for r in outs])
    return (x_all[:Np].reshape(Bp, T, D), x_all[Np:].reshape(Bs, Ts, D),
            st("nsa_p"), st("nsa_s"), st("diff_p"), st("diff_s"), st("mla_p"), st("mla_s"), st("win_p"), st("win_s"))
```
